```python
import jax, jax.numpy as jnp
from jax import lax
import numpy as np

D_MODEL = 1024
BATCH = 2
SEQ = 8192
DEPTH = 2

GRID_W = 64
CTX_LEN = 256
N_MIXERS = 2
N_A_LAYERS = (DEPTH + N_MIXERS - 1) // N_MIXERS
N_B_LAYERS = DEPTH // N_MIXERS
EPS = 1e-6
HG_EXPAND = 128
HG_HEADS = D_MODEL // HG_EXPAND
HG_DK = HG_EXPAND
HG_DV = D_MODEL // HG_HEADS
HG_F = HG_HEADS * HG_DK
CHUNK = 64
D_RNN = D_MODEL
LRU_BLOCKS = 4
LRU_BW = D_RNN // LRU_BLOCKS
LRU_CONV_W = 4
LRU_C = 8.0
D_FF = 2816
FFN_CONV_W = 3

kernel_name = "hybrid_hgrn2_rglru_convffn_prefix_dit"


def rmsnorm(x, g):
    xf = x.astype(jnp.float32)
    y = xf * lax.rsqrt(jnp.mean(xf * xf, axis=-1, keepdims=True) + EPS)
    return (y * g).astype(x.dtype)


def modulate(h, shift, scale):
    return h * (1.0 + scale) + shift


def dwconv(x, w, b):
    K = w.shape[0]
    right = (K - 1) // 2
    left = K - 1 - right
    T = x.shape[1]
    xp = jnp.pad(x, ((0, 0), (left, right), (0, 0)))
    return sum(w[k] * xp[:, k:k + T] for k in range(K)) + b


def _flip(t, d, axis):
    return t if d == 0 else jnp.flip(t, axis=axis)


def hgrn2_chunk_scan(q, k, v, logf, s0):
    Bn, H, T, _ = q.shape
    n = T // CHUNK
    rs = lambda t: t.reshape(Bn, H, n, CHUNK, t.shape[-1])
    q, k, v, logf = rs(q), rs(k), rs(v), rs(logf)
    b = jnp.cumsum(logf, axis=3)
    b_ref = b[:, :, :, CHUNK // 2:CHUNK // 2 + 1]
    b_last = b[:, :, :, -1:]
    qe = q * jnp.exp(b - b_ref)
    ke = k * jnp.exp(b_ref - b)
    scores = jnp.einsum('bhnik,bhnjk->bhnij', qe, ke)
    mask = jnp.tril(jnp.ones((CHUNK, CHUNK), dtype=bool))
    scores = jnp.where(mask, scores, 0.0)
    o_intra = jnp.einsum('bhnij,bhnjv->bhniv', scores, v)
    kv = jnp.einsum('bhnjk,bhnjv->bhnkv', k * jnp.exp(b_last - b), v)
    decay = jnp.exp(b_last[:, :, :, 0])

    def step(s, inp):
        dcy, u = inp
        return dcy[..., None] * s + u, s

    s_final, s_start = lax.scan(step, s0, (jnp.moveaxis(decay, 2, 0), jnp.moveaxis(kv, 2, 0)))
    s_start = jnp.moveaxis(s_start, 0, 2)
    o_inter = jnp.einsum('bhnik,bhnkv->bhniv', q * jnp.exp(b), s_start)
    o = (o_intra + o_inter).reshape(Bn, H, T, v.shape[-1])
    return o, s_final


def hgrn2_mixer(h_ctx, h_lat, w_in, lb, g_gain, w_out, need_ctx):
    def prep(h):
        Bn, T, _ = h.shape
        heads = lambda t: t.reshape(Bn, T, HG_HEADS, -1).transpose(0, 2, 1, 3)
        q, zf_fwd, zf_bwd, v, g = jnp.split(h @ w_in, 5, axis=-1)
        q = heads(jax.nn.silu(q))
        v = heads(v)
        logf, k = [], []
        for d, zf in enumerate((zf_fwd, zf_bwd)):
            f = lb[d] + (1.0 - lb[d]) * jax.nn.sigmoid(zf.astype(jnp.float32))
            logf.append(heads(jnp.log(f)))
            k.append(heads(1.0 - f))
        return q, k, v, logf, g

    qc, kc, vc, lfc, gc = prep(h_ctx)
    ql, kl, vl, lfl, gl = prep(h_lat)
    s0 = jnp.zeros((h_lat.shape[0], HG_HEADS, HG_DK, HG_DV), jnp.float32)
    o_c, o_l = 0.0, 0.0
    for d in range(2):
        oc_d, s_ctx = hgrn2_chunk_scan(_flip(qc, d, 2), _flip(kc[d], d, 2), _flip(vc, d, 2),
                                       _flip(lfc[d], d, 2), s0)
        ol_d, _ = hgrn2_chunk_scan(_flip(ql, d, 2), _flip(kl[d], d, 2), _flip(vl, d, 2),
                                   _flip(lfl[d], d, 2), s_ctx)
        o_c = o_c + _flip(oc_d, d, 2)
        o_l = o_l + _flip(ol_d, d, 2)

    def readout(o, g):
        Bn, H, T, DV = o.shape
        o = rmsnorm(o, g_gain).transpose(0, 2, 1, 3).reshape(Bn, T, H * DV)
        return (o * jax.nn.silu(g)) @ w_out

    y_c = readout(o_c, gc) if need_ctx else None
    return y_c, readout(o_l, gl)


def lru_scan(log_a, u, h0):
    a = jnp.exp(log_a)
    u = u.at[:, 0].add(a[:, 0] * h0)

    def combine(l, r):
        al, ul = l
        ar, ur = r
        return al * ar, ar * ul + ur

    _, h = lax.associative_scan(combine, (a, u), axis=1)
    return h


def rglru_mixer(h_ctx, h_lat, w_in, conv_w, conv_b, wa, ba, wx, bx, lam, w_out, need_ctx):
    Bn, T, _ = h_lat.shape
    rows = T // GRID_W
    to_cm = lambda t: t.reshape(Bn, rows, GRID_W, -1).transpose(0, 2, 1, 3).reshape(Bn, T, -1)
    from_cm = lambda t: t.reshape(Bn, GRID_W, rows, -1).transpose(0, 2, 1, 3).reshape(Bn, T, -1)

    def branch(h):
        gate, xr = jnp.split(h @ w_in, 2, axis=-1)
        return gate, dwconv(xr, conv_w, conv_b)

    def gates(xr, d):
        xb = xr.reshape(*xr.shape[:-1], LRU_BLOCKS, LRU_BW)
        r = jax.nn.sigmoid(jnp.einsum('btnk,nkj->btnj', xb, wa[d]).reshape(xr.shape) + ba[d])
        i = jax.nn.sigmoid(jnp.einsum('btnk,nkj->btnj', xb, wx[d]).reshape(xr.shape) + bx[d])
        log_a = -LRU_C * r.astype(jnp.float32) * jax.nn.softplus(-lam[d].astype(jnp.float32))
        mult = jnp.sqrt(-jnp.expm1(2.0 * log_a))
        return log_a, mult * (i * xr)

    gate_c, xc = branch(h_ctx)
    gate_l, xl = branch(to_cm(h_lat))
    h0 = jnp.zeros((Bn, D_RNN), jnp.float32)
    y_c, y_l = 0.0, 0.0
    for d in range(2):
        la_c, u_c = gates(_flip(xc, d, 1), d)
        hc = lru_scan(la_c, u_c, h0)
        la_l, u_l = gates(_flip(xl, d, 1), d)
        hl = lru_scan(la_l, u_l, hc[:, -1])
        y_c = y_c + _flip(hc, d, 1)
        y_l = y_l + _flip(hl, d, 1)
    out_l = from_cm(y_l * jax.nn.gelu(gate_l)) @ w_out
    out_c = (y_c * jax.nn.gelu(gate_c)) @ w_out if need_ctx else None
    return out_c, out_l


def conv_ffn(h, w_up, conv_w, conv_b, w_down):
    u = dwconv(h @ w_up, conv_w, conv_b)
    gate, val = jnp.split(u, 2, axis=-1)
    return (jax.nn.silu(gate) * val) @ w_down


def setup_inputs(seed: int = 0) -> dict:
    key = jax.random.key(seed)
    ks = iter(jax.random.split(key, 32))
    nrm = lambda shape, s: jax.random.normal(next(ks), shape, jnp.float32) * s
    D = D_MODEL
    a0 = jax.random.uniform(next(ks), (N_B_LAYERS, 2, D_RNN), jnp.float32, 0.9, 0.999)
    sig = a0 ** (1.0 / LRU_C)
    lam = jnp.log(sig) - jnp.log1p(-sig)
    return {
        "x": nrm((BATCH, SEQ, D), 1.0),
        "c": nrm((BATCH, D), 1.0),
        "ctx": nrm((BATCH, CTX_LEN, D), 1.0),
        "c_ctx": nrm((D,), 1.0),
        "w_ada": nrm((DEPTH, D, 6 * D), D ** -0.5),
        "b_ada": nrm((DEPTH, 6 * D), 0.02),
        "norm_g": 1.0 + nrm((DEPTH, 2, D), 0.02),
        "hg_w_in": nrm((N_A_LAYERS, D, 5 * D), D ** -0.5),
        "hg_lb_logits": nrm((N_A_LAYERS + 1, 2, HG_F), 0.1),
        "hg_gnorm": 1.0 + nrm((N_A_LAYERS, HG_DV), 0.02),
        "hg_w_out": nrm((N_A_LAYERS, D, D), D ** -0.5),
        "lru_w_in": nrm((N_B_LAYERS, D, 2 * D_RNN), D ** -0.5),
        "lru_conv_w": nrm((N_B_LAYERS, LRU_CONV_W, D_RNN), LRU_CONV_W ** -0.5),
        "lru_conv_b": nrm((N_B_LAYERS, D_RNN), 0.02),
        "lru_wa": nrm((N_B_LAYERS, 2, LRU_BLOCKS, LRU_BW, LRU_BW), LRU_BW ** -0.5),
        "lru_ba": nrm((N_B_LAYERS, 2, D_RNN), 0.02),
        "lru_wx": nrm((N_B_LAYERS, 2, LRU_BLOCKS, LRU_BW, LRU_BW), LRU_BW ** -0.5),
        "lru_bx": nrm((N_B_LAYERS, 2, D_RNN), 0.02),
        "lru_lambda": lam,
        "lru_w_out": nrm((N_B_LAYERS, D_RNN, D), D_RNN ** -0.5),
        "ffn_w_up": nrm((DEPTH, D, 2 * D_FF), D ** -0.5),
        "ffn_conv_w": nrm((DEPTH, FFN_CONV_W, 2 * D_FF), FFN_CONV_W ** -0.5),
        "ffn_conv_b": nrm((DEPTH, 2 * D_FF), 0.02),
        "ffn_w_down": nrm((DEPTH, D_FF, D), D_FF ** -0.5),
        "final_g": 1.0 + nrm((D,), 0.02),
    }


def reference(x, c, ctx, c_ctx, w_ada, b_ada, norm_g, hg_w_in, hg_lb_logits, hg_gnorm, hg_w_out,
              lru_w_in, lru_conv_w, lru_conv_b, lru_wa, lru_ba, lru_wx, lru_bx, lru_lambda, lru_w_out,
              ffn_w_up, ffn_conv_w, ffn_conv_b, ffn_w_down, final_g):
    lower_bounds = jnp.cumsum(jax.nn.softmax(hg_lb_logits.astype(jnp.float32), axis=0), axis=0)
    lat, cx = x, ctx
    sc_lat, sc_ctx = jax.nn.silu(c), jax.nn.silu(c_ctx)
    for l in range(DEPTH):
        last = l == DEPTH - 1
        m_l = (sc_lat @ w_ada[l] + b_ada[l])[:, None, :]
        m_c = sc_ctx @ w_ada[l] + b_ada[l]
        sh1, sc1, g1, sh2, sc2, g2 = jnp.split(m_l, 6, axis=-1)
        csh1, csc1, cg1, csh2, csc2, cg2 = jnp.split(m_c, 6, axis=-1)
        hl = modulate(rmsnorm(lat, norm_g[l, 0]), sh1, sc1)
        hc = modulate(rmsnorm(cx, norm_g[l, 0]), csh1, csc1)
        j = l // N_MIXERS
        if l % N_MIXERS == 0:
            yc, yl = hgrn2_mixer(hc, hl, hg_w_in[j], lower_bounds[j], hg_gnorm[j], hg_w_out[j],
                                 not last)
        else:
            yc, yl = rglru_mixer(hc, hl, lru_w_in[j], lru_conv_w[j], lru_conv_b[j], lru_wa[j],
                                 lru_ba[j], lru_wx[j], lru_bx[j], lru_lambda[j], lru_w_out[j],
                                 not last)
        lat = lat + g1 * yl
        hl = modulate(rmsnorm(lat, norm_g[l, 1]), sh2, sc2)
        lat = lat + g2 * conv_ffn(hl, ffn_w_up[l], ffn_conv_w[l], ffn_conv_b[l], ffn_w_down[l])
        if not last:
            cx = cx + cg1 * yc
            hc = modulate(rmsnorm(cx, norm_g[l, 1]), csh2, csc2)
            cx = cx + cg2 * conv_ffn(hc, ffn_w_up[l], ffn_conv_w[l], ffn_conv_b[l], ffn_w_down[l])
    return rmsnorm(lat, final_g)
```

```python
import functools

import jax
import jax.numpy as jnp
from jax import lax
from jax.experimental import pallas as pl
from jax.experimental.pallas import tpu as pltpu

F32 = jnp.float32
BF16 = jnp.bfloat16

EPS = 1e-6
GRID_W = 64
HG_DK = 128
CHUNK = 64
LRU_BLOCKS = 4
LRU_C = 8.0
LRU_CONV_W = 4
FFN_CONV_W = 3
N_MOD = 6

SUBLANES = 8
LANES = 128
BF16_ROWS = 16
MXU_N = 256
VMEM_LIMIT_BYTES = 56 * 1024 * 1024
COND_ROWS = SUBLANES


def _cparams(*sem):
    return pltpu.CompilerParams(dimension_semantics=sem, vmem_limit_bytes=VMEM_LIMIT_BYTES)


def _resident(shape):
    zeros = (0,) * len(shape)
    return pl.BlockSpec(shape, lambda *_: zeros, pipeline_mode=pl.Buffered(1))


def _row_tile(t):
    return 512 if t % 512 == 0 else 256


def _sigmoid(x):
    return 1.0 / (1.0 + jnp.exp(-x))


def _silu(x):
    return x * _sigmoid(x)


def _gelu_tanh(x):
    return 0.5 * x * (1.0 + jnp.tanh(0.7978845608028654 * (x + 0.044715 * (x * x * x))))


def _rms(x, g):
    return x * lax.rsqrt(jnp.mean(x * x, axis=-1, keepdims=True) + EPS) * g


def _mod_row(mod_ref, ctx, n_lat):
    if ctx:
        return mod_ref[0, n_lat:n_lat + 1, :]
    return mod_ref[0, pl.ds(pl.program_id(0), 1), :]


def _mod_spec(layer, k, d):
    return pl.BlockSpec((1, COND_ROWS, d), lambda *_: (layer, 0, k))


def _ada_kernel(ct_ref, w_ref, b_ref, o_ref, *, n_cond):
    ct = ct_ref[...]
    s = _silu(ct)
    w = w_ref[0]
    o_ref[0] = jnp.zeros(o_ref.shape[1:], F32) + b_ref[0]
    for m in range(n_cond):
        o_ref[0, m:m + 1, :] = jnp.sum(w * s[:, m:m + 1], axis=0, keepdims=True) + b_ref[0]


def _ada(cond_t, w_ada, b_ada, n_cond):
    depth, d, n = w_ada.shape
    tn = 1024
    return pl.pallas_call(
        functools.partial(_ada_kernel, n_cond=n_cond),
        out_shape=jax.ShapeDtypeStruct((depth, COND_ROWS, n), F32),
        grid=(depth, n // tn),
        in_specs=[
            pl.BlockSpec((d, COND_ROWS), lambda l, j: (0, 0)),
            pl.BlockSpec((1, d, tn), lambda l, j: (l, 0, j)),
            pl.BlockSpec((1, 1, tn), lambda l, j: (l, 0, j)),
        ],
        out_specs=pl.BlockSpec((1, COND_ROWS, tn), lambda l, j: (l, 0, j)),
        compiler_params=_cparams("parallel", "parallel"),
        name="ada",
    )(cond_t, w_ada, b_ada.reshape(depth, 1, n))


def _hg_in_kernel(x_ref, g_ref, sh_ref, sc_ref, w_ref, lbl_ref, q_ref, lff_ref, lfb_ref, v_ref, gs_ref,
                  *, ctx, n_lat, layer_j):
    d = x_ref.shape[-1]
    sh = _mod_row(sh_ref, ctx, n_lat)
    sc = _mod_row(sc_ref, ctx, n_lat)
    h = (_rms(x_ref[0], g_ref[...]) * (1.0 + sc) + sh).astype(BF16)
    nc = 2 * MXU_N
    for c0 in range(0, 5 * d, nc):
        part, p0 = divmod(c0, d)
        z = jnp.dot(h, w_ref[:, c0:c0 + nc], preferred_element_type=F32)
        cols = slice(p0, p0 + nc)
        if part == 0:
            q_ref[0, :, cols] = _silu(z).astype(BF16)
        elif part in (1, 2):
            lg = lbl_ref[part - 1, :, cols]
            e = jnp.exp(lg - jnp.max(lg, axis=0, keepdims=True))
            lb = jnp.sum(e[:layer_j + 1], axis=0, keepdims=True) / jnp.sum(e, axis=0, keepdims=True)
            f = lb + (1.0 - lb) * _sigmoid(z)
            (lff_ref if part == 1 else lfb_ref)[0, :, cols] = jnp.log(f)
        elif part == 3:
            v_ref[0, :, cols] = z.astype(BF16)
        else:
            gs_ref[0, :, cols] = _silu(z).astype(BF16)


def _hg_in(x, mods, layer, norm_g, w_in, lb_logits, layer_j, ctx, n_lat):
    b, t, d = x.shape
    tm = _row_tile(t)
    tile = pl.BlockSpec((1, tm, d), lambda bi, i: (bi, i, 0))
    sds = lambda dt: jax.ShapeDtypeStruct((b, t, d), dt)
    return pl.pallas_call(
        functools.partial(_hg_in_kernel, ctx=ctx, n_lat=n_lat, layer_j=layer_j),
        out_shape=(sds(BF16), sds(F32), sds(F32), sds(BF16), sds(BF16)),
        grid=(b, t // tm),
        in_specs=[tile, _resident((1, d)), _mod_spec(layer, 0, d), _mod_spec(layer, 1, d),
                  _resident(w_in.shape), _resident(lb_logits.shape)],
        out_specs=(tile, tile, tile, tile, tile),
        compiler_params=_cparams("parallel", "parallel"),
        name="hg_in",
    )(x, norm_g.reshape(1, d), mods, mods, w_in, lb_logits)


def _split3(x):
    hi = x.astype(BF16)
    r = x - hi.astype(F32)
    mid = r.astype(BF16)
    lo = (r - mid.astype(F32)).astype(BF16)
    return hi, mid, lo


def _hg_scan_kernel(qf_ref, lf_ref, vf_ref, qb_ref, lb_ref, vb_ref, s0_ref, of_ref, ob_ref, sfin_ref, st_ref,
                    *, n_heads):
    i = pl.program_id(1)
    n_chunks = qf_ref.shape[1] // CHUNK

    @pl.when(i == 0)
    def _():
        st_ref[...] = s0_ref[0]

    row = lax.broadcasted_iota(jnp.int32, (CHUNK, CHUNK), 0)
    col = lax.broadcasted_iota(jnp.int32, (CHUNK, CHUNK), 1)
    dirs = (
        (qf_ref, lf_ref, vf_ref, of_ref, row >= col, CHUNK // 2, CHUNK - 1),
        (qb_ref, lb_ref, vb_ref, ob_ref, row <= col, CHUNK // 2 - 1, 0),
    )

    def chunk_body(c, carry):
        for dn, (q_ref, lfd_ref, v_ref, o_ref, keep, ref_row, last_row) in enumerate(dirs):
            cc = c if dn == 0 else n_chunks - 1 - c
            rows = pl.ds(pl.multiple_of(cc * CHUNK, CHUNK), CHUNK)
            tri = jnp.where(keep, 1.0, 0.0).astype(BF16)
            lf_all = lfd_ref[0, rows, :]
            b_all = sum(jnp.dot(tri, part, preferred_element_type=F32) for part in _split3(lf_all))
            for h in range(n_heads):
                cols = slice(h * HG_DK, (h + 1) * HG_DK)
                bh = b_all[:, cols]
                b_mid = bh[ref_row:ref_row + 1, :]
                b_end = bh[last_row:last_row + 1, :]
                qh = q_ref[0, rows, cols].astype(F32)
                kh = 1.0 - jnp.exp(lf_all[:, cols])
                vh = v_ref[0, rows, cols]
                qe = (qh * jnp.exp(bh - b_mid)).astype(BF16)
                ke = (kh * jnp.exp(b_mid - bh)).astype(BF16)
                kd = (kh * jnp.exp(b_end - bh)).astype(BF16)
                qs = (qh * jnp.exp(bh)).astype(BF16)
                scores = lax.dot_general(qe, ke, (((1,), (1,)), ((), ())), preferred_element_type=F32)
                scores = jnp.where(keep, scores, 0.0).astype(BF16)
                st = st_ref[dn, h]
                o = jnp.dot(scores, vh, preferred_element_type=F32)
                o = o + lax.dot_general(qs, st.astype(BF16), (((1,), (1,)), ((), ())),
                                        preferred_element_type=F32)
                kv_t = lax.dot_general(vh, kd, (((0,), (0,)), ((), ())), preferred_element_type=F32)
                st_ref[dn, h] = st * jnp.exp(b_end) + kv_t
                o_ref[0, rows, cols] = o.astype(BF16)
        return carry

    lax.fori_loop(0, n_chunks, chunk_body, 0)

    @pl.when(i == pl.num_programs(1) - 1)
    def _():
        sfin_ref[0] = st_ref[...]


def _hg_scan(q, lf_f, lf_b, v, s0):
    b, t, d = q.shape
    n_heads = d // HG_DK
    tb = 256
    nblk = t // tb
    fwd = pl.BlockSpec((1, tb, d), lambda bi, i: (bi, i, 0))
    bwd = pl.BlockSpec((1, tb, d), lambda bi, i: (bi, nblk - 1 - i, 0))
    st_spec = pl.BlockSpec((1,) + s0.shape[1:], lambda bi, i: (bi, 0, 0, 0, 0))
    return pl.pallas_call(
        functools.partial(_hg_scan_kernel, n_heads=n_heads),
        out_shape=(jax.ShapeDtypeStruct((b, t, d), BF16), jax.ShapeDtypeStruct((b, t, d), BF16),
                   jax.ShapeDtypeStruct(s0.shape, F32)),
        grid=(b, nblk),
        in_specs=[fwd, fwd, fwd, bwd, bwd, bwd, st_spec],
        out_specs=(fwd, bwd, st_spec),
        scratch_shapes=[pltpu.VMEM(s0.shape[1:], F32)],
        compiler_params=_cparams("parallel", "arbitrary"),
        name="hg_scan",
    )(q, lf_f, v, q, lf_b, v, s0)


def _hg_out_kernel(of_ref, ob_ref, gs_ref, x_ref, gain_ref, w_ref, gate_ref, out_ref, y_ref, *, ctx, n_lat):
    d = x_ref.shape[-1]
    dv = gain_ref.shape[-1]
    gain = gain_ref[...]
    for c0 in range(0, d, dv):
        cols = slice(c0, c0 + dv)
        o = of_ref[0, :, cols].astype(F32) + ob_ref[0, :, cols].astype(F32)
        y_ref[:, cols] = (_rms(o, gain) * gs_ref[0, :, cols].astype(F32)).astype(BF16)
    z = jnp.dot(y_ref[...], w_ref[...], preferred_element_type=F32)
    out_ref[0] = x_ref[0] + _mod_row(gate_ref, ctx, n_lat) * z


def _hg_out(o_f, o_b, gs, x, mods, layer, gain, w_out, ctx, n_lat):
    b, t, d = x.shape
    tm = _row_tile(t)
    tile = pl.BlockSpec((1, tm, d), lambda bi, i: (bi, i, 0))
    return pl.pallas_call(
        functools.partial(_hg_out_kernel, ctx=ctx, n_lat=n_lat),
        out_shape=jax.ShapeDtypeStruct((b, t, d), F32),
        grid=(b, t // tm),
        in_specs=[tile, tile, tile, tile, _resident((1, gain.shape[-1])), _resident(w_out.shape),
                  _mod_spec(layer, 2, d)],
        out_specs=tile,
        scratch_shapes=[pltpu.VMEM((tm, d), BF16)],
        compiler_params=_cparams("parallel", "parallel"),
        name="hg_out",
    )(o_f, o_b, gs, x, gain.reshape(1, -1), w_out, mods)


def _lru_in_kernel(x_ref, g_ref, sh_ref, sc_ref, w_ref, gg_ref, xr_ref, *, ctx, n_lat):
    d = x_ref.shape[-1]
    sh = _mod_row(sh_ref, ctx, n_lat)
    sc = _mod_row(sc_ref, ctx, n_lat)
    h = (_rms(x_ref[0], g_ref[...]) * (1.0 + sc) + sh).astype(BF16)
    nc = 2 * MXU_N
    for c0 in range(0, 2 * d, nc):
        part, p0 = divmod(c0, d)
        z = jnp.dot(h, w_ref[:, c0:c0 + nc], preferred_element_type=F32)
        if part == 0:
            gg_ref[0, :, p0:p0 + nc] = _gelu_tanh(z).astype(BF16)
        else:
            xr_ref[0, :, p0:p0 + nc] = z


def _lru_in(x, mods, layer, norm_g, w_in, ctx, n_lat):
    b, t, d = x.shape
    tm = _row_tile(t)
    tile = pl.BlockSpec((1, tm, d), lambda bi, i: (bi, i, 0))
    return pl.pallas_call(
        functools.partial(_lru_in_kernel, ctx=ctx, n_lat=n_lat),
        out_shape=(jax.ShapeDtypeStruct((b, t, d), BF16), jax.ShapeDtypeStruct((b, t, d), F32)),
        grid=(b, t // tm),
        in_specs=[tile, _resident((1, d)), _mod_spec(layer, 0, d), _mod_spec(layer, 1, d),
                  _resident(w_in.shape)],
        out_specs=(tile, tile),
        compiler_params=_cparams("parallel", "parallel"),
        name="lru_in",
    )(x, norm_g.reshape(1, d), mods, mods, w_in)


LRU_SEG = 128
HALO = SUBLANES


def _softplus(x):
    y = jnp.exp(-jnp.abs(x))
    u = 1.0 + y
    log1p = jnp.where(u == 1.0, y, jnp.log(u) * (y / (u - 1.0)))
    return jnp.maximum(x, 0.0) + log1p


def _lru_scan_kernel(xf_ref, xfp_ref, xfn_ref, xb_ref, xbp_ref, xbn_ref, cw_ref, cb_ref, wa_ref, wx_ref,
                     ba_ref, bx_ref, lam_ref, h0_ref, hf_ref, hb_ref, hfin_ref, buf_ref, a_ref, u_ref,
                     carry_ref):
    w = pl.program_id(0)
    nw = pl.num_programs(0)
    nb, seg, c = xf_ref.shape
    bw = c // LRU_BLOCKS
    n_slab = seg // SUBLANES

    @pl.when(w == 0)
    def _():
        carry_ref[...] = h0_ref[...]

    sub = lax.broadcasted_iota(jnp.int32, (n_slab, SUBLANES, c), 1)
    dirs = ((xf_ref, xfp_ref, xfn_ref, hf_ref, w), (xb_ref, xbp_ref, xbn_ref, hb_ref, nw - 1 - w))
    for dn, (xm_ref, xp_ref, xn_ref, h_ref, seg_idx) in enumerate(dirs):
        buf_ref[:, 0:HALO, :] = jnp.where(seg_idx > 0, xp_ref[...], 0.0)
        buf_ref[:, HALO:HALO + seg, :] = xm_ref[...]
        buf_ref[:, HALO + seg:, :] = jnp.where(seg_idx < nw - 1, xn_ref[...], 0.0)
        left = LRU_CONV_W - 1 - (LRU_CONV_W - 1) // 2
        xc = cb_ref[...] + sum(
            cw_ref[k:k + 1, :] * buf_ref[:, HALO - left + k:HALO - left + k + seg, :]
            for k in range(LRU_CONV_W))
        xc = xc.reshape(nb * seg, c)
        xcb = xc.astype(BF16)
        r_pre = jnp.concatenate(
            [jnp.dot(xcb[:, n * bw:(n + 1) * bw], wa_ref[dn, n], preferred_element_type=F32)
             for n in range(LRU_BLOCKS)], axis=-1)
        i_pre = jnp.concatenate(
            [jnp.dot(xcb[:, n * bw:(n + 1) * bw], wx_ref[dn, n], preferred_element_type=F32)
             for n in range(LRU_BLOCKS)], axis=-1)
        r = _sigmoid(r_pre + ba_ref[dn])
        ig = _sigmoid(i_pre + bx_ref[dn])
        log_a = (-LRU_C) * r * _softplus(-lam_ref[dn])
        a = jnp.exp(log_a)
        mult = jnp.sqrt((1.0 + a * a) * jnp.tanh(-log_a))
        a_ref[...] = a.reshape(nb, seg, c)
        u_ref[...] = (mult * (ig * xc)).reshape(nb, seg, c)

        for bi in range(nb):
            a_s = a_ref[bi].reshape(n_slab, SUBLANES, c)
            u_s = u_ref[bi].reshape(n_slab, SUBLANES, c)
            for s in (1, 2, 4):
                shift = s if dn == 0 else SUBLANES - s
                keep = (sub >= s) if dn == 0 else (sub < SUBLANES - s)
                a_sh = jnp.where(keep, pltpu.roll(a_s, shift, axis=1), 1.0)
                u_sh = jnp.where(keep, pltpu.roll(u_s, shift, axis=1), 0.0)
                u_s = u_s + a_s * u_sh
                a_s = a_s * a_sh
            hp = carry_ref[dn * nb + bi:dn * nb + bi + 1, :]
            order = range(n_slab) if dn == 0 else range(n_slab - 1, -1, -1)
            edge = SUBLANES - 1 if dn == 0 else 0
            for j in order:
                hj = u_s[j] + a_s[j] * hp
                h_ref[bi, j * SUBLANES:(j + 1) * SUBLANES, :] = hj.astype(h_ref.dtype)
                hp = hj[edge:edge + 1, :]
            carry_ref[dn * nb + bi:dn * nb + bi + 1, :] = hp
    hfin_ref[...] = carry_ref[...]


def _lru_scan(xr, conv_w, conv_b, wa, wx, ba, bx, lam, h0, column_major):
    b, t, c = xr.shape
    seg = LRU_SEG
    per = seg // HALO
    if column_major:
        nw = GRID_W
        view = xr.reshape(b, t // nw, nw * c)
        main = lambda col: pl.BlockSpec((b, seg, c), lambda w: (0, 0, col(w)))
        prev = lambda col: pl.BlockSpec((b, HALO, c), lambda w: (0, per - 1, jnp.maximum(col(w) - 1, 0)))
        nxt = lambda col: pl.BlockSpec((b, HALO, c), lambda w: (0, 0, jnp.minimum(col(w) + 1, nw - 1)))
    else:
        nw = t // seg
        view = xr
        main = lambda col: pl.BlockSpec((b, seg, c), lambda w: (0, col(w), 0))
        prev = lambda col: pl.BlockSpec((b, HALO, c), lambda w: (0, jnp.maximum(col(w) * per - 1, 0), 0))
        nxt = lambda col: pl.BlockSpec(
            (b, HALO, c), lambda w: (0, jnp.minimum((col(w) + 1) * per, nw * per - 1), 0))
    up = lambda w: w
    down = lambda w: nw - 1 - w
    hf, hb, hfin = pl.pallas_call(
        _lru_scan_kernel,
        out_shape=(jax.ShapeDtypeStruct(view.shape, BF16), jax.ShapeDtypeStruct(view.shape, BF16),
                   jax.ShapeDtypeStruct(h0.shape, F32)),
        grid=(nw,),
        in_specs=[main(up), prev(up), nxt(up), main(down), prev(down), nxt(down),
                  _resident(conv_w.shape), _resident((1, c)), _resident(wa.shape), _resident(wx.shape),
                  _resident((2, 1, c)), _resident((2, 1, c)), _resident((2, 1, c)), _resident(h0.shape)],
        out_specs=(main(up), main(down), pl.BlockSpec(h0.shape, lambda w: (0, 0))),
        scratch_shapes=[pltpu.VMEM((b, seg + 2 * HALO, c), F32), pltpu.VMEM((b, seg, c), F32),
                        pltpu.VMEM((b, seg, c), F32), pltpu.VMEM(h0.shape, F32)],
        compiler_params=_cparams("arbitrary"),
        name="lru_scan",
    )(view, view, view, view, view, view, conv_w, conv_b.reshape(1, c), wa, wx,
      ba.reshape(2, 1, c), bx.reshape(2, 1, c), lam.reshape(2, 1, c), h0)
    return hf.reshape(b, t, c), hb.reshape(b, t, c), hfin


def _lru_out_kernel(hf_ref, hb_ref, gg_ref, x_ref, w_ref, gate_ref, out_ref, *, ctx, n_lat):
    y = (hf_ref[0].astype(F32) + hb_ref[0].astype(F32)) * gg_ref[0].astype(F32)
    z = jnp.dot(y.astype(BF16), w_ref[...], preferred_element_type=F32)
    out_ref[0] = x_ref[0] + _mod_row(gate_ref, ctx, n_lat) * z


def _lru_out(hf, hb, gg, x, mods, layer, w_out, ctx, n_lat):
    b, t, d = x.shape
    tm = _row_tile(t)
    tile = pl.BlockSpec((1, tm, d), lambda bi, i: (bi, i, 0))
    return pl.pallas_call(
        functools.partial(_lru_out_kernel, ctx=ctx, n_lat=n_lat),
        out_shape=jax.ShapeDtypeStruct((b, t, d), F32),
        grid=(b, t // tm),
        in_specs=[tile, tile, tile, tile, _resident(w_out.shape), _mod_spec(layer, 2, d)],
        out_specs=tile,
        compiler_params=_cparams("parallel", "parallel"),
        name="lru_out",
    )(hf, hb, gg, x, w_out, mods)


def _ffn_kernel(xm_ref, xp_ref, xn_ref, g_ref, sh_ref, sc_ref, gate_ref, wup_ref, cw_ref, cb_ref, wdn_ref,
                fg_ref, out_ref, h_ref, act_ref, *, ctx, n_lat, final_norm):
    i = pl.program_id(1)
    tm = xm_ref.shape[1]
    dff = wdn_ref.shape[0]
    g = g_ref[...]
    sh = _mod_row(sh_ref, ctx, n_lat)
    sc = _mod_row(sc_ref, ctx, n_lat)
    mod = lambda x: _rms(x, g) * (1.0 + sc) + sh
    hp = jnp.where(i > 0, mod(xp_ref[0]), 0.0)
    hn = jnp.where(i < pl.num_programs(1) - 1, mod(xn_ref[0]), 0.0)
    h_ref[...] = jnp.concatenate([hp, mod(xm_ref[0]), hn], axis=0).astype(BF16)
    rows = tm + 2 * HALO
    nc = MXU_N
    for c0 in range(0, dff, nc):
        halves = []
        for base in (c0, dff + c0):
            u = jnp.dot(h_ref[...], wup_ref[:, base:base + nc], preferred_element_type=F32)
            cols = slice(base, base + nc)
            conv = (cb_ref[:, cols]
                    + cw_ref[0:1, cols] * pltpu.roll(u, 1, axis=0)[HALO:HALO + tm]
                    + cw_ref[1:2, cols] * u[HALO:HALO + tm]
                    + cw_ref[2:3, cols] * pltpu.roll(u, rows - 1, axis=0)[HALO:HALO + tm])
            halves.append(conv)
        act_ref[:, c0:c0 + nc] = (_silu(halves[0]) * halves[1]).astype(BF16)
    z = jnp.dot(act_ref[...], wdn_ref[...], preferred_element_type=F32)
    y = xm_ref[0] + _mod_row(gate_ref, ctx, n_lat) * z
    out_ref[0] = _rms(y, fg_ref[...]) if final_norm else y


def _ffn(x, mods, layer, norm_g, w_up, conv_w, conv_b, w_down, final_g, ctx, n_lat, final_norm):
    b, t, d = x.shape
    dff = w_down.shape[0]
    tm = _row_tile(t)
    per = tm // HALO
    tile = pl.BlockSpec((1, tm, d), lambda bi, i: (bi, i, 0))
    prev = pl.BlockSpec((1, HALO, d), lambda bi, i: (bi, jnp.maximum(i * per - 1, 0), 0))
    nxt = pl.BlockSpec((1, HALO, d), lambda bi, i: (bi, jnp.minimum((i + 1) * per, t // HALO - 1), 0))
    return pl.pallas_call(
        functools.partial(_ffn_kernel, ctx=ctx, n_lat=n_lat, final_norm=final_norm),
        out_shape=jax.ShapeDtypeStruct((b, t, d), F32),
        grid=(b, t // tm),
        in_specs=[tile, prev, nxt, _resident((1, d)), _mod_spec(layer, 3, d), _mod_spec(layer, 4, d),
                  _mod_spec(layer, 5, d), _resident(w_up.shape), _resident(conv_w.shape),
                  _resident((1, 2 * dff)), _resident(w_down.shape), _resident((1, d))],
        out_specs=tile,
        scratch_shapes=[pltpu.VMEM((tm + 2 * HALO, d), BF16), pltpu.VMEM((tm, dff), BF16)],
        compiler_params=_cparams("parallel", "parallel"),
        name="ffn",
    )(x, x, x, norm_g.reshape(1, d), mods, mods, mods, w_up, conv_w, conv_b.reshape(1, 2 * dff), w_down,
      final_g.reshape(1, d))


def kernel(x, c, ctx, c_ctx, w_ada, b_ada, norm_g, hg_w_in, hg_lb_logits, hg_gnorm, hg_w_out, lru_w_in,
           lru_conv_w, lru_conv_b, lru_wa, lru_ba, lru_wx, lru_bx, lru_lambda, lru_w_out, ffn_w_up,
           ffn_conv_w, ffn_conv_b, ffn_w_down, final_g):
    nb, _, d = x.shape
    depth = w_ada.shape[0]
    n_mixers = 2
    bf = lambda w: w.astype(BF16)

    cond = jnp.concatenate([c, c_ctx[None, :], jnp.zeros((COND_ROWS - nb - 1, d), F32)], axis=0)
    mods = _ada(cond.T, w_ada, b_ada, nb + 1)

    lat, cx = x, ctx
    for l in range(depth):
        last = l == depth - 1
        j = l // n_mixers
        if l % n_mixers == 0:
            w_in, w_out = bf(hg_w_in[j]), bf(hg_w_out[j])
            n_heads = d // HG_DK
            s0 = jnp.zeros((nb, 2, n_heads, d // n_heads, HG_DK), F32)
            lb_logits = jnp.swapaxes(hg_lb_logits, 0, 1)
            pc = _hg_in(cx, mods, l, norm_g[l, 0], w_in, lb_logits, j, True, nb)
            oc_f, oc_b, s_ctx = _hg_scan(pc[0], pc[1], pc[2], pc[3], s0)
            pl_ = _hg_in(lat, mods, l, norm_g[l, 0], w_in, lb_logits, j, False, nb)
            ol_f, ol_b, _ = _hg_scan(pl_[0], pl_[1], pl_[2], pl_[3], s_ctx)
            lat = _hg_out(ol_f, ol_b, pl_[4], lat, mods, l, hg_gnorm[j], w_out, False, nb)
            if not last:
                cx = _hg_out(oc_f, oc_b, pc[4], cx, mods, l, hg_gnorm[j], w_out, True, nb)
        else:
            w_in, w_out = bf(lru_w_in[j]), bf(lru_w_out[j])
            wa, wx = bf(lru_wa[j]), bf(lru_wx[j])
            scan = functools.partial(_lru_scan, conv_w=lru_conv_w[j], conv_b=lru_conv_b[j], wa=wa, wx=wx,
                                     ba=lru_ba[j], bx=lru_bx[j], lam=lru_lambda[j])
            gg_c, xr_c = _lru_in(cx, mods, l, norm_g[l, 0], w_in, True, nb)
            hc_f, hc_b, h_ctx = scan(xr_c, h0=jnp.zeros((2 * nb, d), F32), column_major=False)
            gg_l, xr_l = _lru_in(lat, mods, l, norm_g[l, 0], w_in, False, nb)
            hl_f, hl_b, _ = scan(xr_l, h0=h_ctx, column_major=True)
            lat = _lru_out(hl_f, hl_b, gg_l, lat, mods, l, w_out, False, nb)
            if not last:
                cx = _lru_out(hc_f, hc_b, gg_c, cx, mods, l, w_out, True, nb)
        ffn = functools.partial(_ffn, mods=mods, layer=l, norm_g=norm_g[l, 1], w_up=bf(ffn_w_up[l]),
                                conv_w=ffn_conv_w[l], conv_b=ffn_conv_b[l], w_down=bf(ffn_w_down[l]),
                                final_g=final_g, n_lat=nb)
        lat = ffn(lat, ctx=False, final_norm=last)
        if not last:
            cx = ffn(cx, ctx=True, final_norm=False)
    return lat
```

```python
import functools

import jax
import jax.numpy as jnp
from jax import lax
from jax.experimental import pallas as pl
from jax.experimental.pallas import tpu as pltpu

F32 = jnp.float32
BF16 = jnp.bfloat16

EPS = 1e-6
GRID_W = 64
HG_DK = 128
CHUNK = 64
LRU_BLOCKS = 4
LRU_C = 8.0
LRU_CONV_W = 4
FFN_CONV_W = 3
N_MOD = 6

SUBLANES = 8
LANES = 128
BF16_ROWS = 16
MXU_N = 256
VMEM_LIMIT_BYTES = 56 * 1024 * 1024
COND_ROWS = SUBLANES


def _cparams(*sem):
    return pltpu.CompilerParams(dimension_semantics=sem, vmem_limit_bytes=VMEM_LIMIT_BYTES)


def _resident(shape):
    zeros = (0,) * len(shape)
    return pl.BlockSpec(shape, lambda *_: zeros, pipeline_mode=pl.Buffered(1))


def _row_tile(t):
    return 512 if t % 512 == 0 else 256


def _sigmoid(x):
    return 0.5 + 0.5 * jnp.tanh(0.5 * x)


def _silu(x):
    return x * _sigmoid(x)


def _gelu_tanh(x):
    return 0.5 * x * (1.0 + jnp.tanh(0.7978845608028654 * (x + 0.044715 * (x * x * x))))


def _rms(x, g):
    return x * lax.rsqrt(jnp.mean(x * x, axis=-1, keepdims=True) + EPS) * g


def _mod_row(mod_ref, ctx, n_lat):
    if ctx:
        return mod_ref[0, n_lat:n_lat + 1, :]
    return mod_ref[0, pl.ds(pl.program_id(0), 1), :]


def _mod_spec(layer, k, d):
    return pl.BlockSpec((1, COND_ROWS, d), lambda *_: (layer, 0, k))


def _ada_kernel(ct_ref, w_ref, b_ref, o_ref, *, n_cond):
    ct = ct_ref[...]
    s = _silu(ct)
    w = w_ref[0]
    o_ref[0] = jnp.zeros(o_ref.shape[1:], F32) + b_ref[0]
    for m in range(n_cond):
        o_ref[0, m:m + 1, :] = jnp.sum(w * s[:, m:m + 1], axis=0, keepdims=True) + b_ref[0]


def _ada(cond_t, w_ada, b_ada, n_cond):
    depth, d, n = w_ada.shape
    tn = 1024
    return pl.pallas_call(
        functools.partial(_ada_kernel, n_cond=n_cond),
        out_shape=jax.ShapeDtypeStruct((depth, COND_ROWS, n), F32),
        grid=(depth, n // tn),
        in_specs=[
            pl.BlockSpec((d, COND_ROWS), lambda l, j: (0, 0)),
            pl.BlockSpec((1, d, tn), lambda l, j: (l, 0, j)),
            pl.BlockSpec((1, 1, tn), lambda l, j: (l, 0, j)),
        ],
        out_specs=pl.BlockSpec((1, COND_ROWS, tn), lambda l, j: (l, 0, j)),
        compiler_params=_cparams("parallel", "parallel"),
        name="ada",
    )(cond_t, w_ada, b_ada.reshape(depth, 1, n))


def _hg_in_kernel(x_ref, g_ref, sh_ref, sc_ref, w_ref, lbl_ref, q_ref, lff_ref, lfb_ref, v_ref, gs_ref,
                  *, ctx, n_lat, layer_j):
    d = x_ref.shape[-1]
    sh = _mod_row(sh_ref, ctx, n_lat)
    sc = _mod_row(sc_ref, ctx, n_lat)
    h = (_rms(x_ref[0], g_ref[...]) * (1.0 + sc) + sh).astype(BF16)
    nc = 2 * MXU_N
    for c0 in range(0, 5 * d, nc):
        part, p0 = divmod(c0, d)
        z = jnp.dot(h, w_ref[:, c0:c0 + nc], preferred_element_type=F32)
        cols = slice(p0, p0 + nc)
        if part == 0:
            q_ref[0, :, cols] = _silu(z).astype(BF16)
        elif part in (1, 2):
            lg = lbl_ref[part - 1, :, cols]
            e = jnp.exp(lg - jnp.max(lg, axis=0, keepdims=True))
            lb = jnp.sum(e[:layer_j + 1], axis=0, keepdims=True) / jnp.sum(e, axis=0, keepdims=True)
            f = lb + (1.0 - lb) * _sigmoid(z)
            (lff_ref if part == 1 else lfb_ref)[0, :, cols] = jnp.log(f)
        elif part == 3:
            v_ref[0, :, cols] = z.astype(BF16)
        else:
            gs_ref[0, :, cols] = _silu(z).astype(BF16)


def _hg_in(x, mods, layer, norm_g, w_in, lb_logits, layer_j, ctx, n_lat):
    b, t, d = x.shape
    tm = _row_tile(t)
    tile = pl.BlockSpec((1, tm, d), lambda bi, i: (bi, i, 0))
    sds = lambda dt: jax.ShapeDtypeStruct((b, t, d), dt)
    return pl.pallas_call(
        functools.partial(_hg_in_kernel, ctx=ctx, n_lat=n_lat, layer_j=layer_j),
        out_shape=(sds(BF16), sds(F32), sds(F32), sds(BF16), sds(BF16)),
        grid=(b, t // tm),
        in_specs=[tile, _resident((1, d)), _mod_spec(layer, 0, d), _mod_spec(layer, 1, d),
                  _resident(w_in.shape), _resident(lb_logits.shape)],
        out_specs=(tile, tile, tile, tile, tile),
        compiler_params=_cparams("parallel", "parallel"),
        name="hg_in",
    )(x, norm_g.reshape(1, d), mods, mods, w_in, lb_logits)


def _split3(x):
    hi = x.astype(BF16)
    r = x - hi.astype(F32)
    mid = r.astype(BF16)
    lo = (r - mid.astype(F32)).astype(BF16)
    return hi, mid, lo


def _hg_scan_kernel(qf_ref, lf_ref, vf_ref, qb_ref, lb_ref, vb_ref, s0_ref, of_ref, ob_ref, sfin_ref, st_ref,
                    *, n_heads):
    i = pl.program_id(1)
    n_chunks = qf_ref.shape[1] // CHUNK

    @pl.when(i == 0)
    def _():
        st_ref[...] = s0_ref[0]

    row = lax.broadcasted_iota(jnp.int32, (CHUNK, CHUNK), 0)
    col = lax.broadcasted_iota(jnp.int32, (CHUNK, CHUNK), 1)
    dirs = (
        (qf_ref, lf_ref, vf_ref, of_ref, row >= col, CHUNK // 2, CHUNK - 1),
        (qb_ref, lb_ref, vb_ref, ob_ref, row <= col, CHUNK // 2 - 1, 0),
    )

    def chunk_body(c, carry):
        for dn, (q_ref, lfd_ref, v_ref, o_ref, keep, ref_row, last_row) in enumerate(dirs):
            cc = c if dn == 0 else n_chunks - 1 - c
            rows = pl.ds(pl.multiple_of(cc * CHUNK, CHUNK), CHUNK)
            tri = jnp.where(keep, 1.0, 0.0).astype(BF16)
            lf_all = lfd_ref[0, rows, :]
            b_all = sum(jnp.dot(tri, part, preferred_element_type=F32) for part in _split3(lf_all))
            for h in range(n_heads):
                cols = slice(h * HG_DK, (h + 1) * HG_DK)
                bh = b_all[:, cols]
                b_mid = bh[ref_row:ref_row + 1, :]
                b_end = bh[last_row:last_row + 1, :]
                qh = q_ref[0, rows, cols].astype(F32)
                kh = 1.0 - jnp.exp(lf_all[:, cols])
                vh = v_ref[0, rows, cols]
                qe = (qh * jnp.exp(bh - b_mid)).astype(BF16)
                ke = (kh * jnp.exp(b_mid - bh)).astype(BF16)
                kd = (kh * jnp.exp(b_end - bh)).astype(BF16)
                qs = (qh * jnp.exp(bh)).astype(BF16)
                scores = lax.dot_general(qe, ke, (((1,), (1,)), ((), ())), preferred_element_type=F32)
                scores = jnp.where(keep, scores, 0.0).astype(BF16)
                st = st_ref[dn, h]
                o = jnp.dot(scores, vh, preferred_element_type=F32)
                o = o + lax.dot_general(qs, st.astype(BF16), (((1,), (1,)), ((), ())),
                                        preferred_element_type=F32)
                kv_t = lax.dot_general(vh, kd, (((0,), (0,)), ((), ())), preferred_element_type=F32)
                st_ref[dn, h] = st * jnp.exp(b_end) + kv_t
                o_ref[0, rows, cols] = o.astype(BF16)
        return carry

    lax.fori_loop(0, n_chunks, chunk_body, 0)

    @pl.when(i == pl.num_programs(1) - 1)
    def _():
        sfin_ref[0] = st_ref[...]


def _hg_scan(q, lf_f, lf_b, v, s0):
    b, t, d = q.shape
    n_heads = d // HG_DK
    tb = 256
    nblk = t // tb
    fwd = pl.BlockSpec((1, tb, d), lambda bi, i: (bi, i, 0))
    bwd = pl.BlockSpec((1, tb, d), lambda bi, i: (bi, nblk - 1 - i, 0))
    st_spec = pl.BlockSpec((1,) + s0.shape[1:], lambda bi, i: (bi, 0, 0, 0, 0))
    return pl.pallas_call(
        functools.partial(_hg_scan_kernel, n_heads=n_heads),
        out_shape=(jax.ShapeDtypeStruct((b, t, d), BF16), jax.ShapeDtypeStruct((b, t, d), BF16),
                   jax.ShapeDtypeStruct(s0.shape, F32)),
        grid=(b, nblk),
        in_specs=[fwd, fwd, fwd, bwd, bwd, bwd, st_spec],
        out_specs=(fwd, bwd, st_spec),
        scratch_shapes=[pltpu.VMEM(s0.shape[1:], F32)],
        compiler_params=_cparams("parallel", "arbitrary"),
        name="hg_scan",
    )(q, lf_f, v, q, lf_b, v, s0)


def _hg_out_kernel(of_ref, ob_ref, gs_ref, x_ref, gain_ref, w_ref, gate_ref, out_ref, y_ref, *, ctx, n_lat):
    d = x_ref.shape[-1]
    dv = gain_ref.shape[-1]
    gain = gain_ref[...]
    for c0 in range(0, d, dv):
        cols = slice(c0, c0 + dv)
        o = of_ref[0, :, cols].astype(F32) + ob_ref[0, :, cols].astype(F32)
        y_ref[:, cols] = (_rms(o, gain) * gs_ref[0, :, cols].astype(F32)).astype(BF16)
    z = jnp.dot(y_ref[...], w_ref[...], preferred_element_type=F32)
    out_ref[0] = x_ref[0] + _mod_row(gate_ref, ctx, n_lat) * z


def _hg_out(o_f, o_b, gs, x, mods, layer, gain, w_out, ctx, n_lat):
    b, t, d = x.shape
    tm = _row_tile(t)
    tile = pl.BlockSpec((1, tm, d), lambda bi, i: (bi, i, 0))
    return pl.pallas_call(
        functools.partial(_hg_out_kernel, ctx=ctx, n_lat=n_lat),
        out_shape=jax.ShapeDtypeStruct((b, t, d), F32),
        grid=(b, t // tm),
        in_specs=[tile, tile, tile, tile, _resident((1, gain.shape[-1])), _resident(w_out.shape),
                  _mod_spec(layer, 2, d)],
        out_specs=tile,
        scratch_shapes=[pltpu.VMEM((tm, d), BF16)],
        compiler_params=_cparams("parallel", "parallel"),
        name="hg_out",
    )(o_f, o_b, gs, x, gain.reshape(1, -1), w_out, mods)


def _lru_in_kernel(x_ref, g_ref, sh_ref, sc_ref, w_ref, gg_ref, xr_ref, *, ctx, n_lat):
    d = x_ref.shape[-1]
    sh = _mod_row(sh_ref, ctx, n_lat)
    sc = _mod_row(sc_ref, ctx, n_lat)
    h = (_rms(x_ref[0], g_ref[...]) * (1.0 + sc) + sh).astype(BF16)
    nc = 2 * MXU_N
    for c0 in range(0, 2 * d, nc):
        part, p0 = divmod(c0, d)
        z = jnp.dot(h, w_ref[:, c0:c0 + nc], preferred_element_type=F32)
        if part == 0:
            gg_ref[0, :, p0:p0 + nc] = _gelu_tanh(z).astype(BF16)
        else:
            xr_ref[0, :, p0:p0 + nc] = z


def _lru_in(x, mods, layer, norm_g, w_in, ctx, n_lat):
    b, t, d = x.shape
    tm = _row_tile(t)
    tile = pl.BlockSpec((1, tm, d), lambda bi, i: (bi, i, 0))
    return pl.pallas_call(
        functools.partial(_lru_in_kernel, ctx=ctx, n_lat=n_lat),
        out_shape=(jax.ShapeDtypeStruct((b, t, d), BF16), jax.ShapeDtypeStruct((b, t, d), F32)),
        grid=(b, t // tm),
        in_specs=[tile, _resident((1, d)), _mod_spec(layer, 0, d), _mod_spec(layer, 1, d),
                  _resident(w_in.shape)],
        out_specs=(tile, tile),
        compiler_params=_cparams("parallel", "parallel"),
        name="lru_in",
    )(x, norm_g.reshape(1, d), mods, mods, w_in)


LRU_COL_BLOCK = BF16_ROWS
LRU_GATE_ROWS = MXU_N
LRU_SCAN_SEGS = 2
LOG2_E = 1.4426950408889634
LN_2 = 0.6931471805599453
HALO = SUBLANES


def _softplus(x):
    y = jnp.exp(-jnp.abs(x))
    u = 1.0 + y
    log1p = jnp.where(u == 1.0, y, jnp.log(u) * (y / (u - 1.0)))
    return jnp.maximum(x, 0.0) + log1p


def _lru_scan_kernel(xf_ref, xfp_ref, xfn_ref, xb_ref, xbp_ref, xbn_ref, cw_ref, cb_ref, wa_ref, wx_ref,
                     ba_ref, bx_ref, lam_ref, h0_ref, hf_ref, hb_ref, hfin_ref, xpad_ref, a_ref, u_ref,
                     hl_ref, pr_ref, carry_ref):
    j = pl.program_id(2)
    nj = pl.num_programs(2)
    _, nr, wb, cb = xf_ref.shape

    @pl.when(j == 0)
    def _():
        carry_ref[...] = h0_ref[:, 0]

    col = lax.broadcasted_iota(jnp.int32, (wb, cb), 0)
    left = LRU_CONV_W - 1 - (LRU_CONV_W - 1) // 2
    right = LRU_CONV_W - 1 - left
    dirs = ((xf_ref, xfp_ref, xfn_ref, j), (xb_ref, xbp_ref, xbn_ref, nj - 1 - j))
    for dn, (x_ref, xp_ref, xn_ref, jb) in enumerate(dirs):
        for k in range(left):
            edge = jnp.where(jb > 0, xp_ref[0, HALO - left + k, wb - 1:wb, :], 0.0)
            xpad_ref[dn, k] = jnp.where(col == 0, edge, pltpu.roll(x_ref[0, nr - left + k], 1, axis=0))
        xpad_ref[dn, left:left + nr] = x_ref[0]
        for k in range(right):
            edge = jnp.where(jb < nj - 1, xn_ref[0, k, 0:1, :], 0.0)
            xpad_ref[dn, left + nr + k] = jnp.where(col == wb - 1, edge,
                                                    pltpu.roll(x_ref[0, k], wb - 1, axis=0))

    cw = [cw_ref[k:k + 1, :] for k in range(LRU_CONV_W)]
    cbias = cb_ref[...]
    k2 = [(-0.5 * LRU_C * LOG2_E) * _softplus(-lam_ref[dn]) for dn in range(2)]
    rc = LRU_GATE_ROWS // wb

    def gate_chunk(ci, carry):
        r0 = pl.multiple_of(ci * rc, rc)
        for dn in range(2):
            xc = cbias + sum(cw[k] * xpad_ref[dn, pl.ds(r0 + k, rc)] for k in range(LRU_CONV_W))
            xc = xc.reshape(rc * wb, cb)
            xcb = xc.astype(BF16)
            tr = jnp.tanh(0.5 * (jnp.dot(xcb, wa_ref[dn, 0], preferred_element_type=F32) + ba_ref[dn]))
            ti = jnp.tanh(0.5 * (jnp.dot(xcb, wx_ref[dn, 0], preferred_element_type=F32) + bx_ref[dn]))
            log2_a = k2[dn] * tr + k2[dn]
            a = jnp.exp2(log2_a)
            m2 = (a * a + 1.0) * jnp.tanh(log2_a * (-LN_2))
            mult = jnp.where(m2 > 0.0, m2 * lax.rsqrt(m2), 0.0)
            hx = 0.5 * xc
            a_ref[dn, pl.ds(r0, rc)] = a.reshape(rc, wb, cb)
            u_ref[dn, pl.ds(r0, rc)] = (mult * (hx * ti + hx)).reshape(rc, wb, cb)
        return carry

    lax.fori_loop(0, nr // rc, gate_chunk, 0)

    n_seg = LRU_SCAN_SEGS
    seg = nr // n_seg

    def scan_step(rr, carry):
        out = []
        for dn in range(2):
            for s in range(n_seg):
                h, p = carry[dn * n_seg + s]
                r = s * seg + (rr if dn == 0 else seg - 1 - rr)
                a = a_ref[dn, r]
                h = a * h + u_ref[dn, r]
                p = a * p
                hl_ref[dn, r] = h
                pr_ref[dn, r] = p
                out.append((h, p))
        return tuple(out)

    zero = jnp.zeros((wb, cb), F32)
    one = jnp.ones((wb, cb), F32)
    ends = lax.fori_loop(0, seg, scan_step, ((zero, one),) * (2 * n_seg), unroll=2)

    for dn, o_ref in enumerate((hf_ref, hb_ref)):
        c_in = carry_ref[dn]
        c_units = [zero] * n_seg
        wls = range(wb) if dn == 0 else range(wb - 1, -1, -1)
        segs = range(n_seg) if dn == 0 else range(n_seg - 1, -1, -1)
        for wl in wls:
            for s in segs:
                h_end, p_end = ends[dn * n_seg + s]
                c_units[s] = jnp.where(col == wl, c_in, c_units[s])
                c_in = p_end[wl:wl + 1, :] * c_in + h_end[wl:wl + 1, :]
        carry_ref[dn] = c_in
        for s in range(n_seg):
            rows = slice(s * seg, (s + 1) * seg)
            o_ref[0, rows] = (hl_ref[dn, rows] + pr_ref[dn, rows] * c_units[s]).astype(o_ref.dtype)

    @pl.when(j == nj - 1)
    def _():
        hfin_ref[:, 0] = carry_ref[...]


def _lru_scan(xr, conv_w, conv_b, wa, wx, ba, bx, lam, h0, grid_w):
    b, t, c = xr.shape
    nr = t // grid_w
    cb = c // LRU_BLOCKS
    wb = LRU_COL_BLOCK
    nj = grid_w // wb
    x4 = xr.reshape(b, nr, grid_w, c)
    up = lambda j: j
    down = lambda j: nj - 1 - j
    main = lambda cj: pl.BlockSpec((1, nr, wb, cb), lambda bi, n, j: (bi, 0, cj(j), n))
    prev = lambda cj: pl.BlockSpec((1, HALO, wb, cb),
                                   lambda bi, n, j: (bi, nr // HALO - 1, jnp.maximum(cj(j) - 1, 0), n))
    nxt = lambda cj: pl.BlockSpec((1, HALO, wb, cb),
                                  lambda bi, n, j: (bi, 0, jnp.minimum(cj(j) + 1, nj - 1), n))
    chan = lambda rows: pl.BlockSpec((rows, cb), lambda bi, n, j: (0, n))
    per_dir = pl.BlockSpec((2, 1, cb), lambda bi, n, j: (0, 0, n))
    gate_w = pl.BlockSpec((2, 1, cb, cb), lambda bi, n, j: (0, n, 0, 0))
    state = pl.BlockSpec((2, 1, 1, cb), lambda bi, n, j: (0, bi, 0, n))
    hf, hb, hfin = pl.pallas_call(
        _lru_scan_kernel,
        out_shape=(jax.ShapeDtypeStruct(x4.shape, BF16), jax.ShapeDtypeStruct(x4.shape, BF16),
                   jax.ShapeDtypeStruct(h0.shape, F32)),
        grid=(b, LRU_BLOCKS, nj),
        in_specs=[main(up), prev(up), nxt(up), main(down), prev(down), nxt(down),
                  chan(LRU_CONV_W), chan(1), gate_w, gate_w, per_dir, per_dir, per_dir, state],
        out_specs=(main(up), main(down), state),
        scratch_shapes=[pltpu.VMEM((2, nr + LRU_CONV_W - 1, wb, cb), F32)]
        + [pltpu.VMEM((2, nr, wb, cb), F32)] * 4 + [pltpu.VMEM((2, 1, cb), F32)],
        compiler_params=_cparams("parallel", "parallel", "arbitrary"),
        name="lru_scan",
    )(x4, x4, x4, x4, x4, x4, conv_w, conv_b.reshape(1, c), wa, wx,
      ba.reshape(2, 1, c), bx.reshape(2, 1, c), lam.reshape(2, 1, c), h0)
    return hf.reshape(b, t, c), hb.reshape(b, t, c), hfin


def _lru_out_kernel(hf_ref, hb_ref, gg_ref, x_ref, w_ref, gate_ref, out_ref, *, ctx, n_lat):
    y = (hf_ref[0].astype(F32) + hb_ref[0].astype(F32)) * gg_ref[0].astype(F32)
    z = jnp.dot(y.astype(BF16), w_ref[...], preferred_element_type=F32)
    out_ref[0] = x_ref[0] + _mod_row(gate_ref, ctx, n_lat) * z


def _lru_out(hf, hb, gg, x, mods, layer, w_out, ctx, n_lat):
    b, t, d = x.shape
    tm = _row_tile(t)
    tile = pl.BlockSpec((1, tm, d), lambda bi, i: (bi, i, 0))
    return pl.pallas_call(
        functools.partial(_lru_out_kernel, ctx=ctx, n_lat=n_lat),
        out_shape=jax.ShapeDtypeStruct((b, t, d), F32),
        grid=(b, t // tm),
        in_specs=[tile, tile, tile, tile, _resident(w_out.shape), _mod_spec(layer, 2, d)],
        out_specs=tile,
        compiler_params=_cparams("parallel", "parallel"),
        name="lru_out",
    )(hf, hb, gg, x, w_out, mods)


def _ffn_kernel(xm_ref, xp_ref, xn_ref, g_ref, sh_ref, sc_ref, gate_ref, wup_ref, cw_ref, cb_ref, wdn_ref,
                fg_ref, out_ref, h_ref, act_ref, *, ctx, n_lat, final_norm):
    i = pl.program_id(1)
    tm = xm_ref.shape[1]
    dff = wdn_ref.shape[0]
    g = g_ref[...]
    sh = _mod_row(sh_ref, ctx, n_lat)
    sc = _mod_row(sc_ref, ctx, n_lat)
    mod = lambda x: _rms(x, g) * (1.0 + sc) + sh
    hp = jnp.where(i > 0, mod(xp_ref[0]), 0.0)
    hn = jnp.where(i < pl.num_programs(1) - 1, mod(xn_ref[0]), 0.0)
    h_ref[...] = jnp.concatenate([hp, mod(xm_ref[0]), hn], axis=0).astype(BF16)
    rows = tm + 2 * HALO
    nc = MXU_N
    for c0 in range(0, dff, nc):
        halves = []
        for base in (c0, dff + c0):
            u = jnp.dot(h_ref[...], wup_ref[:, base:base + nc], preferred_element_type=F32)
            cols = slice(base, base + nc)
            conv = (cb_ref[:, cols]
                    + cw_ref[0:1, cols] * pltpu.roll(u, 1, axis=0)[HALO:HALO + tm]
                    + cw_ref[1:2, cols] * u[HALO:HALO + tm]
                    + cw_ref[2:3, cols] * pltpu.roll(u, rows - 1, axis=0)[HALO:HALO + tm])
            halves.append(conv)
        act_ref[:, c0:c0 + nc] = (_silu(halves[0]) * halves[1]).astype(BF16)
    z = jnp.dot(act_ref[...], wdn_ref[...], preferred_element_type=F32)
    y = xm_ref[0] + _mod_row(gate_ref, ctx, n_lat) * z
    out_ref[0] = _rms(y, fg_ref[...]) if final_norm else y


def _ffn(x, mods, layer, norm_g, w_up, conv_w, conv_b, w_down, final_g, ctx, n_lat, final_norm):
    b, t, d = x.shape
    dff = w_down.shape[0]
    tm = _row_tile(t)
    per = tm // HALO
    tile = pl.BlockSpec((1, tm, d), lambda bi, i: (bi, i, 0))
    prev = pl.BlockSpec((1, HALO, d), lambda bi, i: (bi, jnp.maximum(i * per - 1, 0), 0))
    nxt = pl.BlockSpec((1, HALO, d), lambda bi, i: (bi, jnp.minimum((i + 1) * per, t // HALO - 1), 0))
    return pl.pallas_call(
        functools.partial(_ffn_kernel, ctx=ctx, n_lat=n_lat, final_norm=final_norm),
        out_shape=jax.ShapeDtypeStruct((b, t, d), F32),
        grid=(b, t // tm),
        in_specs=[tile, prev, nxt, _resident((1, d)), _mod_spec(layer, 3, d), _mod_spec(layer, 4, d),
                  _mod_spec(layer, 5, d), _resident(w_up.shape), _resident(conv_w.shape),
                  _resident((1, 2 * dff)), _resident(w_down.shape), _resident((1, d))],
        out_specs=tile,
        scratch_shapes=[pltpu.VMEM((tm + 2 * HALO, d), BF16), pltpu.VMEM((tm, dff), BF16)],
        compiler_params=_cparams("parallel", "parallel"),
        name="ffn",
    )(x, x, x, norm_g.reshape(1, d), mods, mods, mods, w_up, conv_w, conv_b.reshape(1, 2 * dff), w_down,
      final_g.reshape(1, d))


def kernel(x, c, ctx, c_ctx, w_ada, b_ada, norm_g, hg_w_in, hg_lb_logits, hg_gnorm, hg_w_out, lru_w_in,
           lru_conv_w, lru_conv_b, lru_wa, lru_ba, lru_wx, lru_bx, lru_lambda, lru_w_out, ffn_w_up,
           ffn_conv_w, ffn_conv_b, ffn_w_down, final_g):
    nb, _, d = x.shape
    depth = w_ada.shape[0]
    n_mixers = 2
    bf = lambda w: w.astype(BF16)

    cond = jnp.concatenate([c, c_ctx[None, :], jnp.zeros((COND_ROWS - nb - 1, d), F32)], axis=0)
    mods = _ada(cond.T, w_ada, b_ada, nb + 1)

    lat, cx = x, ctx
    for l in range(depth):
        last = l == depth - 1
        j = l // n_mixers
        if l % n_mixers == 0:
            w_in, w_out = bf(hg_w_in[j]), bf(hg_w_out[j])
            n_heads = d // HG_DK
            s0 = jnp.zeros((nb, 2, n_heads, d // n_heads, HG_DK), F32)
            lb_logits = jnp.swapaxes(hg_lb_logits, 0, 1)
            pc = _hg_in(cx, mods, l, norm_g[l, 0], w_in, lb_logits, j, True, nb)
            oc_f, oc_b, s_ctx = _hg_scan(pc[0], pc[1], pc[2], pc[3], s0)
            pl_ = _hg_in(lat, mods, l, norm_g[l, 0], w_in, lb_logits, j, False, nb)
            ol_f, ol_b, _ = _hg_scan(pl_[0], pl_[1], pl_[2], pl_[3], s_ctx)
            lat = _hg_out(ol_f, ol_b, pl_[4], lat, mods, l, hg_gnorm[j], w_out, False, nb)
            if not last:
                cx = _hg_out(oc_f, oc_b, pc[4], cx, mods, l, hg_gnorm[j], w_out, True, nb)
        else:
            w_in, w_out = bf(lru_w_in[j]), bf(lru_w_out[j])
            wa, wx = bf(lru_wa[j]), bf(lru_wx[j])
            scan = functools.partial(_lru_scan, conv_w=lru_conv_w[j], conv_b=lru_conv_b[j], wa=wa, wx=wx,
                                     ba=lru_ba[j], bx=lru_bx[j], lam=lru_lambda[j])
            wc = LRU_COL_BLOCK
            t_ctx = cx.shape[1]
            flip = lambda a: jnp.swapaxes(a.reshape(nb, wc, t_ctx // wc, d), 1, 2).reshape(nb, t_ctx, d)
            unflip = lambda a: jnp.swapaxes(a.reshape(nb, t_ctx // wc, wc, d), 1, 2).reshape(nb, t_ctx, d)
            gg_c, xr_c = _lru_in(cx, mods, l, norm_g[l, 0], w_in, True, nb)
            hc_f, hc_b, h_ctx = scan(flip(xr_c), h0=jnp.zeros((2, nb, 1, d), F32), grid_w=wc)
            gg_l, xr_l = _lru_in(lat, mods, l, norm_g[l, 0], w_in, False, nb)
            hl_f, hl_b, _ = scan(xr_l, h0=h_ctx, grid_w=GRID_W)
            lat = _lru_out(hl_f, hl_b, gg_l, lat, mods, l, w_out, False, nb)
            if not last:
                cx = _lru_out(unflip(hc_f), unflip(hc_b), gg_c, cx, mods, l, w_out, True, nb)
        ffn = functools.partial(_ffn, mods=mods, layer=l, norm_g=norm_g[l, 1], w_up=bf(ffn_w_up[l]),
                                conv_w=ffn_conv_w[l], conv_b=ffn_conv_b[l], w_down=bf(ffn_w_down[l]),
                                final_g=final_g, n_lat=nb)
        lat = ffn(lat, ctx=False, final_norm=last)
        if not last:
            cx = ffn(cx, ctx=True, final_norm=False)
    return lat
```

```python
import functools

import jax
import jax.numpy as jnp
from jax import lax
from jax.experimental import pallas as pl
from jax.experimental.pallas import tpu as pltpu

F32 = jnp.float32
BF16 = jnp.bfloat16

EPS = 1e-6
GRID_W = 64
HG_DK = 128
CHUNK = 64
LRU_BLOCKS = 4
LRU_C = 8.0
LRU_CONV_W = 4
FFN_CONV_W = 3
N_MOD = 6

SUBLANES = 8
LANES = 128
BF16_ROWS = 16
MXU_N = 256
VMEM_LIMIT_BYTES = 56 * 1024 * 1024
COND_ROWS = SUBLANES


def _cparams(*sem):
    return pltpu.CompilerParams(dimension_semantics=sem, vmem_limit_bytes=VMEM_LIMIT_BYTES)


def _resident(shape):
    zeros = (0,) * len(shape)
    return pl.BlockSpec(shape, lambda *_: zeros, pipeline_mode=pl.Buffered(1))


def _row_tile(t, largest=512):
    tm = largest
    while t % tm:
        tm //= 2
    return tm


def _sigmoid(x):
    return 0.5 + 0.5 * jnp.tanh(0.5 * x)


def _silu(x):
    return x * _sigmoid(x)


def _gelu_tanh(x):
    return 0.5 * x * (1.0 + jnp.tanh(0.7978845608028654 * (x + 0.044715 * (x * x * x))))


def _rms(x, g):
    return x * lax.rsqrt(jnp.mean(x * x, axis=-1, keepdims=True) + EPS) * g


def _mod_row(mod_ref, ctx, n_lat):
    if ctx:
        return mod_ref[0, n_lat:n_lat + 1, :]
    return mod_ref[0, pl.ds(pl.program_id(0), 1), :]


def _mod_spec(layer, k, d):
    return pl.BlockSpec((1, COND_ROWS, d), lambda *_: (layer, 0, k))


def _ada_kernel(ct_ref, w_ref, b_ref, o_ref, *, n_cond):
    ct = ct_ref[...]
    s = _silu(ct)
    w = w_ref[0]
    o_ref[0] = jnp.zeros(o_ref.shape[1:], F32) + b_ref[0]
    for m in range(n_cond):
        o_ref[0, m:m + 1, :] = jnp.sum(w * s[:, m:m + 1], axis=0, keepdims=True) + b_ref[0]


def _ada(cond_t, w_ada, b_ada, n_cond):
    depth, d, n = w_ada.shape
    tn = 1024
    return pl.pallas_call(
        functools.partial(_ada_kernel, n_cond=n_cond),
        out_shape=jax.ShapeDtypeStruct((depth, COND_ROWS, n), F32),
        grid=(depth, n // tn),
        in_specs=[
            pl.BlockSpec((d, COND_ROWS), lambda l, j: (0, 0)),
            pl.BlockSpec((1, d, tn), lambda l, j: (l, 0, j)),
            pl.BlockSpec((1, 1, tn), lambda l, j: (l, 0, j)),
        ],
        out_specs=pl.BlockSpec((1, COND_ROWS, tn), lambda l, j: (l, 0, j)),
        compiler_params=_cparams("parallel", "parallel"),
        name="ada",
    )(cond_t, w_ada, b_ada.reshape(depth, 1, n))


def _hg_in_kernel(x_ref, g_ref, sh_ref, sc_ref, w_ref, lbl_ref, q_ref, lff_ref, lfb_ref, v_ref, gs_ref,
                  *, ctx, n_lat, layer_j):
    d = x_ref.shape[-1]
    sh = _mod_row(sh_ref, ctx, n_lat)
    sc = _mod_row(sc_ref, ctx, n_lat)
    h = (_rms(x_ref[0], g_ref[...]) * (1.0 + sc) + sh).astype(BF16)
    nc = 2 * MXU_N
    for c0 in range(0, 5 * d, nc):
        part, p0 = divmod(c0, d)
        z = jnp.dot(h, w_ref[:, c0:c0 + nc], preferred_element_type=F32)
        cols = slice(p0, p0 + nc)
        if part == 0:
            q_ref[0, :, cols] = _silu(z).astype(BF16)
        elif part in (1, 2):
            lg = lbl_ref[part - 1, :, cols]
            e = jnp.exp(lg - jnp.max(lg, axis=0, keepdims=True))
            lb = jnp.sum(e[:layer_j + 1], axis=0, keepdims=True) / jnp.sum(e, axis=0, keepdims=True)
            f = lb + (1.0 - lb) * _sigmoid(z)
            (lff_ref if part == 1 else lfb_ref)[0, :, cols] = jnp.log(f)
        elif part == 3:
            v_ref[0, :, cols] = z.astype(BF16)
        else:
            gs_ref[0, :, cols] = _silu(z).astype(BF16)


def _hg_in(x, mods, layer, norm_g, w_in, lb_logits, layer_j, ctx, n_lat):
    b, t, d = x.shape
    tm = _row_tile(t)
    tile = pl.BlockSpec((1, tm, d), lambda bi, i: (bi, i, 0))
    sds = lambda dt: jax.ShapeDtypeStruct((b, t, d), dt)
    return pl.pallas_call(
        functools.partial(_hg_in_kernel, ctx=ctx, n_lat=n_lat, layer_j=layer_j),
        out_shape=(sds(BF16), sds(F32), sds(F32), sds(BF16), sds(BF16)),
        grid=(b, t // tm),
        in_specs=[tile, _resident((1, d)), _mod_spec(layer, 0, d), _mod_spec(layer, 1, d),
                  _resident(w_in.shape), _resident(lb_logits.shape)],
        out_specs=(tile, tile, tile, tile, tile),
        compiler_params=_cparams("parallel", "parallel"),
        name="hg_in",
    )(x, norm_g.reshape(1, d), mods, mods, w_in, lb_logits)


HG_PAIR = 2 * HG_DK
HG_SCAN_BLOCK = 256


def _split2(x):
    hi = x.astype(BF16)
    return hi, (x - hi.astype(F32)).astype(BF16)


def _hg_scan_kernel(qf_ref, lf_ref, vf_ref, qb_ref, lb_ref, vb_ref, s0_ref, of_ref, ob_ref, sfin_ref,
                    st_ref, st16_ref, qe_ref, qs_ref, kebd_ref, kdbd_ref, vbd_ref, dec_ref):
    i = pl.program_id(1)
    tb, d = qf_ref.shape[1], qf_ref.shape[2]
    n_chunks = tb // CHUNK
    n_pairs = d // HG_PAIR

    @pl.when(i == 0)
    def _():
        st_ref[...] = s0_ref[0]
        st16_ref[...] = s0_ref[0].astype(BF16)
        kebd_ref[...] = jnp.zeros(kebd_ref.shape, BF16)
        kdbd_ref[...] = jnp.zeros(kdbd_ref.shape, BF16)
        vbd_ref[...] = jnp.zeros(vbd_ref.shape, BF16)

    row = lax.broadcasted_iota(jnp.int32, (tb, tb), 0)
    col = lax.broadcasted_iota(jnp.int32, (tb, tb), 1)
    same_chunk = (row // CHUNK) == (col // CHUNK)
    r_in = lax.broadcasted_iota(jnp.int32, (CHUNK, 2 * CHUNK), 0)
    c_in = lax.broadcasted_iota(jnp.int32, (CHUNK, 2 * CHUNK), 1) % CHUNK
    dirs = (
        (qf_ref, lf_ref, vf_ref, of_ref, same_chunk & (row >= col), r_in >= c_in, CHUNK // 2, CHUNK - 1),
        (qb_ref, lb_ref, vb_ref, ob_ref, same_chunk & (row <= col), r_in <= c_in, CHUNK // 2 - 1, 0),
    )

    for dn, (q_ref, lfd_ref, v_ref, _, keep_blk, _, ref_row, last_row) in enumerate(dirs):
        tri = jnp.where(keep_blk, 1.0, 0.0).astype(BF16)
        lf = lfd_ref[0]
        b = sum(jnp.dot(tri, part, preferred_element_type=F32) for part in _split2(lf))
        for c in range(n_chunks):
            rows = slice(c * CHUNK, (c + 1) * CHUNK)
            bc = b[rows]
            b_mid = bc[ref_row:ref_row + 1, :]
            b_end = bc[last_row:last_row + 1, :]
            e_q = jnp.exp(bc - b_mid)
            qe = q_ref[0, rows, :].astype(F32) * e_q
            ke = (1.0 - jnp.exp(lf[rows])) * (1.0 / e_q)
            qe_ref[dn, rows, :] = qe.astype(BF16)
            qs_ref[dn, rows, :] = (qe * jnp.exp(b_mid)).astype(BF16)
            ke16 = ke.astype(BF16)
            kd16 = (ke * jnp.exp(b_end - b_mid)).astype(BF16)
            for h in range(2 * n_pairs):
                p, hh = divmod(h, 2)
                cols = slice(h * HG_DK, (h + 1) * HG_DK)
                blk = (slice(hh * CHUNK, (hh + 1) * CHUNK), slice(hh * HG_DK, (hh + 1) * HG_DK))
                kebd_ref[dn, c, p, blk[0], blk[1]] = ke16[:, cols]
                kdbd_ref[dn, c, p, blk[0], blk[1]] = kd16[:, cols]
                vbd_ref[dn, c, p, blk[0], blk[1]] = v_ref[0, rows, cols]
            dec_ref[dn, c:c + 1, :] = jnp.exp(b_end)

    for c in range(n_chunks):
        for dn, (_, _, v_ref, o_ref, _, keep, _, _) in enumerate(dirs):
            cc = c if dn == 0 else n_chunks - 1 - c
            rows = slice(cc * CHUNK, (cc + 1) * CHUNK)
            for p in range(n_pairs):
                pc = slice(p * HG_PAIR, (p + 1) * HG_PAIR)
                scores = lax.dot_general(qe_ref[dn, rows, pc], kebd_ref[dn, cc, p], (((1,), (1,)), ((), ())),
                                         preferred_element_type=F32)
                scores = jnp.where(keep, scores, 0.0).astype(BF16)
                o = jnp.dot(scores, vbd_ref[dn, cc, p], preferred_element_type=F32)
                o = o + lax.dot_general(qs_ref[dn, rows, pc], st16_ref[dn, p], (((1,), (1,)), ((), ())),
                                        preferred_element_type=F32)
                o_ref[0, rows, pc] = o.astype(BF16)
                v_stack = jnp.concatenate(
                    [v_ref[0, rows, p * HG_PAIR:p * HG_PAIR + HG_DK],
                     v_ref[0, rows, p * HG_PAIR + HG_DK:(p + 1) * HG_PAIR]], axis=0)
                kv = lax.dot_general(v_stack, kdbd_ref[dn, cc, p], (((0,), (0,)), ((), ())),
                                     preferred_element_type=F32)
                for hh in range(2):
                    sl = slice(hh * HG_DK, (hh + 1) * HG_DK)
                    cols = slice(p * HG_PAIR + hh * HG_DK, p * HG_PAIR + (hh + 1) * HG_DK)
                    new = st_ref[dn, p, sl, sl] * dec_ref[dn, cc:cc + 1, cols] + kv[:, sl]
                    st_ref[dn, p, sl, sl] = new
                    st16_ref[dn, p, sl, sl] = new.astype(BF16)

    @pl.when(i == pl.num_programs(1) - 1)
    def _():
        sfin_ref[0] = st_ref[...]


def _hg_scan(q, lf_f, lf_b, v, s0):
    b, t, d = q.shape
    tb = HG_SCAN_BLOCK
    nblk = t // tb
    fwd = pl.BlockSpec((1, tb, d), lambda bi, i: (bi, i, 0))
    bwd = pl.BlockSpec((1, tb, d), lambda bi, i: (bi, nblk - 1 - i, 0))
    st_spec = pl.BlockSpec((1,) + s0.shape[1:], lambda bi, i: (bi, 0, 0, 0, 0))
    blockdiag = pltpu.VMEM((2, tb // CHUNK, d // HG_PAIR, 2 * CHUNK, HG_PAIR), BF16)
    return pl.pallas_call(
        _hg_scan_kernel,
        out_shape=(jax.ShapeDtypeStruct((b, t, d), BF16), jax.ShapeDtypeStruct((b, t, d), BF16),
                   jax.ShapeDtypeStruct(s0.shape, F32)),
        grid=(b, nblk),
        in_specs=[fwd, fwd, fwd, bwd, bwd, bwd, st_spec],
        out_specs=(fwd, bwd, st_spec),
        scratch_shapes=[pltpu.VMEM(s0.shape[1:], F32), pltpu.VMEM(s0.shape[1:], BF16),
                        pltpu.VMEM((2, tb, d), BF16), pltpu.VMEM((2, tb, d), BF16),
                        blockdiag, blockdiag, blockdiag, pltpu.VMEM((2, tb // CHUNK, d), F32)],
        compiler_params=_cparams("parallel", "arbitrary"),
        name="hg_scan",
    )(q, lf_f, v, q, lf_b, v, s0)


def _hg_out_kernel(of_ref, ob_ref, gs_ref, x_ref, gain_ref, w_ref, gate_ref, out_ref, y_ref, *, ctx, n_lat):
    d = x_ref.shape[-1]
    dv = gain_ref.shape[-1]
    gain = gain_ref[...]
    for c0 in range(0, d, dv):
        cols = slice(c0, c0 + dv)
        o = of_ref[0, :, cols].astype(F32) + ob_ref[0, :, cols].astype(F32)
        y_ref[:, cols] = (_rms(o, gain) * gs_ref[0, :, cols].astype(F32)).astype(BF16)
    z = jnp.dot(y_ref[...], w_ref[...], preferred_element_type=F32)
    out_ref[0] = x_ref[0] + _mod_row(gate_ref, ctx, n_lat) * z


def _hg_out(o_f, o_b, gs, x, mods, layer, gain, w_out, ctx, n_lat):
    b, t, d = x.shape
    tm = _row_tile(t)
    tile = pl.BlockSpec((1, tm, d), lambda bi, i: (bi, i, 0))
    return pl.pallas_call(
        functools.partial(_hg_out_kernel, ctx=ctx, n_lat=n_lat),
        out_shape=jax.ShapeDtypeStruct((b, t, d), F32),
        grid=(b, t // tm),
        in_specs=[tile, tile, tile, tile, _resident((1, gain.shape[-1])), _resident(w_out.shape),
                  _mod_spec(layer, 2, d)],
        out_specs=tile,
        scratch_shapes=[pltpu.VMEM((tm, d), BF16)],
        compiler_params=_cparams("parallel", "parallel"),
        name="hg_out",
    )(o_f, o_b, gs, x, gain.reshape(1, -1), w_out, mods)


def _lru_in_kernel(x_ref, g_ref, sh_ref, sc_ref, w_ref, gg_ref, xr_ref, *, ctx, n_lat):
    d = x_ref.shape[-1]
    sh = _mod_row(sh_ref, ctx, n_lat)
    sc = _mod_row(sc_ref, ctx, n_lat)
    h = (_rms(x_ref[0], g_ref[...]) * (1.0 + sc) + sh).astype(BF16)
    nc = 2 * MXU_N
    for c0 in range(0, 2 * d, nc):
        part, p0 = divmod(c0, d)
        z = jnp.dot(h, w_ref[:, c0:c0 + nc], preferred_element_type=F32)
        if part == 0:
            gg_ref[0, :, p0:p0 + nc] = _gelu_tanh(z).astype(BF16)
        else:
            xr_ref[0, :, p0:p0 + nc] = z


def _lru_in(x, mods, layer, norm_g, w_in, ctx, n_lat):
    b, t, d = x.shape
    tm = _row_tile(t)
    tile = pl.BlockSpec((1, tm, d), lambda bi, i: (bi, i, 0))
    return pl.pallas_call(
        functools.partial(_lru_in_kernel, ctx=ctx, n_lat=n_lat),
        out_shape=(jax.ShapeDtypeStruct((b, t, d), BF16), jax.ShapeDtypeStruct((b, t, d), F32)),
        grid=(b, t // tm),
        in_specs=[tile, _resident((1, d)), _mod_spec(layer, 0, d), _mod_spec(layer, 1, d),
                  _resident(w_in.shape)],
        out_specs=(tile, tile),
        compiler_params=_cparams("parallel", "parallel"),
        name="lru_in",
    )(x, norm_g.reshape(1, d), mods, mods, w_in)


LRU_COL_BLOCK = BF16_ROWS
LRU_GATE_ROWS = MXU_N
LRU_SCAN_SEGS = 2
LOG2_E = 1.4426950408889634
LN_2 = 0.6931471805599453
HALO = SUBLANES


def _softplus(x):
    y = jnp.exp(-jnp.abs(x))
    u = 1.0 + y
    log1p = jnp.where(u == 1.0, y, jnp.log(u) * (y / (u - 1.0)))
    return jnp.maximum(x, 0.0) + log1p


def _lru_scan_kernel(xf_ref, xfp_ref, xfn_ref, xb_ref, xbp_ref, xbn_ref, cw_ref, cb_ref, wa_ref, wx_ref,
                     ba_ref, bx_ref, lam_ref, h0_ref, hf_ref, hb_ref, hfin_ref, xpad_ref, a_ref, u_ref,
                     hl_ref, pr_ref, carry_ref):
    j = pl.program_id(2)
    nj = pl.num_programs(2)
    _, nr, wb, cb = xf_ref.shape

    @pl.when(j == 0)
    def _():
        carry_ref[...] = h0_ref[:, 0]

    col = lax.broadcasted_iota(jnp.int32, (wb, cb), 0)
    left = LRU_CONV_W - 1 - (LRU_CONV_W - 1) // 2
    right = LRU_CONV_W - 1 - left
    dirs = ((xf_ref, xfp_ref, xfn_ref, j), (xb_ref, xbp_ref, xbn_ref, nj - 1 - j))
    for dn, (x_ref, xp_ref, xn_ref, jb) in enumerate(dirs):
        for k in range(left):
            edge = jnp.where(jb > 0, xp_ref[0, HALO - left + k, wb - 1:wb, :], 0.0)
            xpad_ref[dn, k] = jnp.where(col == 0, edge, pltpu.roll(x_ref[0, nr - left + k], 1, axis=0))
        xpad_ref[dn, left:left + nr] = x_ref[0]
        for k in range(right):
            edge = jnp.where(jb < nj - 1, xn_ref[0, k, 0:1, :], 0.0)
            xpad_ref[dn, left + nr + k] = jnp.where(col == wb - 1, edge,
                                                    pltpu.roll(x_ref[0, k], wb - 1, axis=0))

    cw = [cw_ref[k:k + 1, :] for k in range(LRU_CONV_W)]
    cbias = cb_ref[...]
    k2 = [(-0.5 * LRU_C * LOG2_E) * _softplus(-lam_ref[dn]) for dn in range(2)]
    rc = LRU_GATE_ROWS // wb

    def gate_chunk(ci, carry):
        r0 = pl.multiple_of(ci * rc, rc)
        for dn in range(2):
            xc = cbias + sum(cw[k] * xpad_ref[dn, pl.ds(r0 + k, rc)] for k in range(LRU_CONV_W))
            xc = xc.reshape(rc * wb, cb)
            xcb = xc.astype(BF16)
            tr = jnp.tanh(0.5 * (jnp.dot(xcb, wa_ref[dn, 0], preferred_element_type=F32) + ba_ref[dn]))
            ti = jnp.tanh(0.5 * (jnp.dot(xcb, wx_ref[dn, 0], preferred_element_type=F32) + bx_ref[dn]))
            log2_a = k2[dn] * tr + k2[dn]
            a = jnp.exp2(log2_a)
            m2 = (a * a + 1.0) * jnp.tanh(log2_a * (-LN_2))
            mult = jnp.where(m2 > 0.0, m2 * lax.rsqrt(m2), 0.0)
            hx = 0.5 * xc
            a_ref[dn, pl.ds(r0, rc)] = a.reshape(rc, wb, cb)
            u_ref[dn, pl.ds(r0, rc)] = (mult * (hx * ti + hx)).reshape(rc, wb, cb)
        return carry

    lax.fori_loop(0, nr // rc, gate_chunk, 0)

    n_seg = LRU_SCAN_SEGS
    seg = nr // n_seg

    def scan_step(rr, carry):
        out = []
        for dn in range(2):
            for s in range(n_seg):
                h, p = carry[dn * n_seg + s]
                r = s * seg + (rr if dn == 0 else seg - 1 - rr)
                a = a_ref[dn, r]
                h = a * h + u_ref[dn, r]
                p = a * p
                hl_ref[dn, r] = h
                pr_ref[dn, r] = p
                out.append((h, p))
        return tuple(out)

    zero = jnp.zeros((wb, cb), F32)
    one = jnp.ones((wb, cb), F32)
    ends = lax.fori_loop(0, seg, scan_step, ((zero, one),) * (2 * n_seg), unroll=2)

    for dn, o_ref in enumerate((hf_ref, hb_ref)):
        c_in = carry_ref[dn]
        c_units = [zero] * n_seg
        wls = range(wb) if dn == 0 else range(wb - 1, -1, -1)
        segs = range(n_seg) if dn == 0 else range(n_seg - 1, -1, -1)
        for wl in wls:
            for s in segs:
                h_end, p_end = ends[dn * n_seg + s]
                c_units[s] = jnp.where(col == wl, c_in, c_units[s])
                c_in = p_end[wl:wl + 1, :] * c_in + h_end[wl:wl + 1, :]
        carry_ref[dn] = c_in
        for s in range(n_seg):
            rows = slice(s * seg, (s + 1) * seg)
            o_ref[0, rows] = (hl_ref[dn, rows] + pr_ref[dn, rows] * c_units[s]).astype(o_ref.dtype)

    @pl.when(j == nj - 1)
    def _():
        hfin_ref[:, 0] = carry_ref[...]


def _lru_scan(xr, conv_w, conv_b, wa, wx, ba, bx, lam, h0, grid_w):
    b, t, c = xr.shape
    nr = t // grid_w
    cb = c // LRU_BLOCKS
    wb = LRU_COL_BLOCK
    nj = grid_w // wb
    x4 = xr.reshape(b, nr, grid_w, c)
    up = lambda j: j
    down = lambda j: nj - 1 - j
    main = lambda cj: pl.BlockSpec((1, nr, wb, cb), lambda bi, n, j: (bi, 0, cj(j), n))
    prev = lambda cj: pl.BlockSpec((1, HALO, wb, cb),
                                   lambda bi, n, j: (bi, nr // HALO - 1, jnp.maximum(cj(j) - 1, 0), n))
    nxt = lambda cj: pl.BlockSpec((1, HALO, wb, cb),
                                  lambda bi, n, j: (bi, 0, jnp.minimum(cj(j) + 1, nj - 1), n))
    chan = lambda rows: pl.BlockSpec((rows, cb), lambda bi, n, j: (0, n))
    per_dir = pl.BlockSpec((2, 1, cb), lambda bi, n, j: (0, 0, n))
    gate_w = pl.BlockSpec((2, 1, cb, cb), lambda bi, n, j: (0, n, 0, 0))
    state = pl.BlockSpec((2, 1, 1, cb), lambda bi, n, j: (0, bi, 0, n))
    hf, hb, hfin = pl.pallas_call(
        _lru_scan_kernel,
        out_shape=(jax.ShapeDtypeStruct(x4.shape, BF16), jax.ShapeDtypeStruct(x4.shape, BF16),
                   jax.ShapeDtypeStruct(h0.shape, F32)),
        grid=(b, LRU_BLOCKS, nj),
        in_specs=[main(up), prev(up), nxt(up), main(down), prev(down), nxt(down),
                  chan(LRU_CONV_W), chan(1), gate_w, gate_w, per_dir, per_dir, per_dir, state],
        out_specs=(main(up), main(down), state),
        scratch_shapes=[pltpu.VMEM((2, nr + LRU_CONV_W - 1, wb, cb), F32)]
        + [pltpu.VMEM((2, nr, wb, cb), F32)] * 4 + [pltpu.VMEM((2, 1, cb), F32)],
        compiler_params=_cparams("parallel", "parallel", "arbitrary"),
        name="lru_scan",
    )(x4, x4, x4, x4, x4, x4, conv_w, conv_b.reshape(1, c), wa, wx,
      ba.reshape(2, 1, c), bx.reshape(2, 1, c), lam.reshape(2, 1, c), h0)
    return hf.reshape(b, t, c), hb.reshape(b, t, c), hfin


def _lru_out_kernel(hf_ref, hb_ref, gg_ref, x_ref, w_ref, gate_ref, out_ref, *, ctx, n_lat):
    y = (hf_ref[0].astype(F32) + hb_ref[0].astype(F32)) * gg_ref[0].astype(F32)
    z = jnp.dot(y.astype(BF16), w_ref[...], preferred_element_type=F32)
    out_ref[0] = x_ref[0] + _mod_row(gate_ref, ctx, n_lat) * z


def _lru_out(hf, hb, gg, x, mods, layer, w_out, ctx, n_lat):
    b, t, d = x.shape
    tm = _row_tile(t)
    tile = pl.BlockSpec((1, tm, d), lambda bi, i: (bi, i, 0))
    return pl.pallas_call(
        functools.partial(_lru_out_kernel, ctx=ctx, n_lat=n_lat),
        out_shape=jax.ShapeDtypeStruct((b, t, d), F32),
        grid=(b, t // tm),
        in_specs=[tile, tile, tile, tile, _resident(w_out.shape), _mod_spec(layer, 2, d)],
        out_specs=tile,
        compiler_params=_cparams("parallel", "parallel"),
        name="lru_out",
    )(hf, hb, gg, x, w_out, mods)


FFN_ROW_TILE = 512

def _ffn_kernel(xm_ref, xp_ref, xn_ref, g_ref, sh_ref, sc_ref, gate_ref, wup_ref, cw_ref, cb_ref, wdn_ref,
                fg_ref, out_ref, h_ref, act_ref, *, ctx, n_lat, final_norm):
    i = pl.program_id(1)
    tm = xm_ref.shape[1]
    dff = wdn_ref.shape[0]
    g = g_ref[...]
    sh = _mod_row(sh_ref, ctx, n_lat)
    sc = _mod_row(sc_ref, ctx, n_lat)
    mod = lambda x: _rms(x, g) * (1.0 + sc) + sh
    hp = jnp.where(i > 0, mod(xp_ref[0]), 0.0)
    hn = jnp.where(i < pl.num_programs(1) - 1, mod(xn_ref[0]), 0.0)
    h_ref[...] = jnp.concatenate([hp, mod(xm_ref[0]), hn], axis=0).astype(BF16)
    rows = tm + 2 * HALO
    nc = MXU_N
    for c0 in range(0, dff, nc):
        halves = []
        for base in (c0, dff + c0):
            u = jnp.dot(h_ref[...], wup_ref[:, base:base + nc], preferred_element_type=F32)
            cols = slice(base, base + nc)
            conv = (cb_ref[:, cols]
                    + cw_ref[0:1, cols] * pltpu.roll(u, 1, axis=0)[HALO:HALO + tm]
                    + cw_ref[1:2, cols] * u[HALO:HALO + tm]
                    + cw_ref[2:3, cols] * pltpu.roll(u, rows - 1, axis=0)[HALO:HALO + tm])
            halves.append(conv)
        act_ref[:, c0:c0 + nc] = (_silu(halves[0]) * halves[1]).astype(BF16)
    z = jnp.dot(act_ref[...], wdn_ref[...], preferred_element_type=F32)
    y = xm_ref[0] + _mod_row(gate_ref, ctx, n_lat) * z
    out_ref[0] = _rms(y, fg_ref[...]) if final_norm else y


def _ffn(x, mods, layer, norm_g, w_up, conv_w, conv_b, w_down, final_g, ctx, n_lat, final_norm):
    b, t, d = x.shape
    dff = w_down.shape[0]
    tm = _row_tile(t, FFN_ROW_TILE)
    per = tm // HALO
    tile = pl.BlockSpec((1, tm, d), lambda bi, i: (bi, i, 0))
    prev = pl.BlockSpec((1, HALO, d), lambda bi, i: (bi, jnp.maximum(i * per - 1, 0), 0))
    nxt = pl.BlockSpec((1, HALO, d), lambda bi, i: (bi, jnp.minimum((i + 1) * per, t // HALO - 1), 0))
    return pl.pallas_call(
        functools.partial(_ffn_kernel, ctx=ctx, n_lat=n_lat, final_norm=final_norm),
        out_shape=jax.ShapeDtypeStruct((b, t, d), F32),
        grid=(b, t // tm),
        in_specs=[tile, prev, nxt, _resident((1, d)), _mod_spec(layer, 3, d), _mod_spec(layer, 4, d),
                  _mod_spec(layer, 5, d), _resident(w_up.shape), _resident(conv_w.shape),
                  _resident((1, 2 * dff)), _resident(w_down.shape), _resident((1, d))],
        out_specs=tile,
        scratch_shapes=[pltpu.VMEM((tm + 2 * HALO, d), BF16), pltpu.VMEM((tm, dff), BF16)],
        compiler_params=_cparams("parallel", "parallel"),
        name="ffn",
    )(x, x, x, norm_g.reshape(1, d), mods, mods, mods, w_up, conv_w, conv_b.reshape(1, 2 * dff), w_down,
      final_g.reshape(1, d))


def kernel(x, c, ctx, c_ctx, w_ada, b_ada, norm_g, hg_w_in, hg_lb_logits, hg_gnorm, hg_w_out, lru_w_in,
           lru_conv_w, lru_conv_b, lru_wa, lru_ba, lru_wx, lru_bx, lru_lambda, lru_w_out, ffn_w_up,
           ffn_conv_w, ffn_conv_b, ffn_w_down, final_g):
    nb, _, d = x.shape
    depth = w_ada.shape[0]
    n_mixers = 2
    bf = lambda w: w.astype(BF16)

    cond = jnp.concatenate([c, c_ctx[None, :], jnp.zeros((COND_ROWS - nb - 1, d), F32)], axis=0)
    mods = _ada(cond.T, w_ada, b_ada, nb + 1)

    lat, cx = x, ctx
    for l in range(depth):
        last = l == depth - 1
        j = l // n_mixers
        if l % n_mixers == 0:
            w_in, w_out = bf(hg_w_in[j]), bf(hg_w_out[j])
            s0 = jnp.zeros((nb, 2, d // HG_PAIR, HG_PAIR, HG_PAIR), F32)
            lb_logits = jnp.swapaxes(hg_lb_logits, 0, 1)
            pc = _hg_in(cx, mods, l, norm_g[l, 0], w_in, lb_logits, j, True, nb)
            oc_f, oc_b, s_ctx = _hg_scan(pc[0], pc[1], pc[2], pc[3], s0)
            pl_ = _hg_in(lat, mods, l, norm_g[l, 0], w_in, lb_logits, j, False, nb)
            ol_f, ol_b, _ = _hg_scan(pl_[0], pl_[1], pl_[2], pl_[3], s_ctx)
            lat = _hg_out(ol_f, ol_b, pl_[4], lat, mods, l, hg_gnorm[j], w_out, False, nb)
            if not last:
                cx = _hg_out(oc_f, oc_b, pc[4], cx, mods, l, hg_gnorm[j], w_out, True, nb)
        else:
            w_in, w_out = bf(lru_w_in[j]), bf(lru_w_out[j])
            wa, wx = bf(lru_wa[j]), bf(lru_wx[j])
            scan = functools.partial(_lru_scan, conv_w=lru_conv_w[j], conv_b=lru_conv_b[j], wa=wa, wx=wx,
                                     ba=lru_ba[j], bx=lru_bx[j], lam=lru_lambda[j])
            wc = LRU_COL_BLOCK
            t_ctx = cx.shape[1]
            flip = lambda a: jnp.swapaxes(a.reshape(nb, wc, t_ctx // wc, d), 1, 2).reshape(nb, t_ctx, d)
            unflip = lambda a: jnp.swapaxes(a.reshape(nb, t_ctx // wc, wc, d), 1, 2).reshape(nb, t_ctx, d)
            gg_c, xr_c = _lru_in(cx, mods, l, norm_g[l, 0], w_in, True, nb)
            hc_f, hc_b, h_ctx = scan(flip(xr_c), h0=jnp.zeros((2, nb, 1, d), F32), grid_w=wc)
            gg_l, xr_l = _lru_in(lat, mods, l, norm_g[l, 0], w_in, False, nb)
            hl_f, hl_b, _ = scan(xr_l, h0=h_ctx, grid_w=GRID_W)
            lat = _lru_out(hl_f, hl_b, gg_l, lat, mods, l, w_out, False, nb)
            if not last:
                cx = _lru_out(unflip(hc_f), unflip(hc_b), gg_c, cx, mods, l, w_out, True, nb)
        ffn = functools.partial(_ffn, mods=mods, layer=l, norm_g=norm_g[l, 1], w_up=bf(ffn_w_up[l]),
                                conv_w=ffn_conv_w[l], conv_b=ffn_conv_b[l], w_down=bf(ffn_w_down[l]),
                                final_g=final_g, n_lat=nb)
        lat = ffn(lat, ctx=False, final_norm=last)
        if not last:
            cx = ffn(cx, ctx=True, final_norm=False)
    return lat
```

```python
import functools

import jax
import jax.numpy as jnp
from jax import lax
from jax.experimental import pallas as pl
from jax.experimental.pallas import tpu as pltpu

F32 = jnp.float32
BF16 = jnp.bfloat16

EPS = 1e-6
GRID_W = 64
HG_DK = 128
CHUNK = 64
LRU_BLOCKS = 4
LRU_C = 8.0
LRU_CONV_W = 4
FFN_CONV_W = 3
N_MOD = 6

SUBLANES = 8
LANES = 128
BF16_ROWS = 16
MXU_N = 256
VMEM_LIMIT_BYTES = 56 * 1024 * 1024
COND_ROWS = SUBLANES


def _cparams(*sem):
    return pltpu.CompilerParams(dimension_semantics=sem, vmem_limit_bytes=VMEM_LIMIT_BYTES)


def _resident(shape):
    zeros = (0,) * len(shape)
    return pl.BlockSpec(shape, lambda *_: zeros, pipeline_mode=pl.Buffered(1))


def _row_tile(t, largest=512):
    tm = largest
    while t % tm:
        tm //= 2
    return tm


def _sigmoid(x):
    return 0.5 + 0.5 * jnp.tanh(0.5 * x)


def _silu(x):
    return x * _sigmoid(x)


def _gelu_tanh(x):
    return 0.5 * x * (1.0 + jnp.tanh(0.7978845608028654 * (x + 0.044715 * (x * x * x))))


def _rms(x, g):
    return x * lax.rsqrt(jnp.mean(x * x, axis=-1, keepdims=True) + EPS) * g


def _mod_row(mod_ref, ctx, n_lat):
    if ctx:
        return mod_ref[0, n_lat:n_lat + 1, :]
    return mod_ref[0, pl.ds(pl.program_id(0), 1), :]


def _mod_spec(layer, k, d):
    return pl.BlockSpec((1, COND_ROWS, d), lambda *_: (layer, 0, k))


def _ada_kernel(ct_ref, w_ref, b_ref, o_ref, *, n_cond):
    ct = ct_ref[...]
    s = _silu(ct)
    w = w_ref[0]
    o_ref[0] = jnp.zeros(o_ref.shape[1:], F32) + b_ref[0]
    for m in range(n_cond):
        o_ref[0, m:m + 1, :] = jnp.sum(w * s[:, m:m + 1], axis=0, keepdims=True) + b_ref[0]


def _ada(cond_t, w_ada, b_ada, n_cond):
    depth, d, n = w_ada.shape
    tn = 1024
    return pl.pallas_call(
        functools.partial(_ada_kernel, n_cond=n_cond),
        out_shape=jax.ShapeDtypeStruct((depth, COND_ROWS, n), F32),
        grid=(depth, n // tn),
        in_specs=[
            pl.BlockSpec((d, COND_ROWS), lambda l, j: (0, 0)),
            pl.BlockSpec((1, d, tn), lambda l, j: (l, 0, j)),
            pl.BlockSpec((1, 1, tn), lambda l, j: (l, 0, j)),
        ],
        out_specs=pl.BlockSpec((1, COND_ROWS, tn), lambda l, j: (l, 0, j)),
        compiler_params=_cparams("parallel", "parallel"),
        name="ada",
    )(cond_t, w_ada, b_ada.reshape(depth, 1, n))


def _hg_in_kernel(x_ref, g_ref, sh_ref, sc_ref, w_ref, lbl_ref, q_ref, lff_ref, lfb_ref, v_ref, gs_ref,
                  *, ctx, n_lat, layer_j):
    d = x_ref.shape[-1]
    sh = _mod_row(sh_ref, ctx, n_lat)
    sc = _mod_row(sc_ref, ctx, n_lat)
    h = (_rms(x_ref[0], g_ref[...]) * (1.0 + sc) + sh).astype(BF16)
    nc = 2 * MXU_N
    for c0 in range(0, 5 * d, nc):
        part, p0 = divmod(c0, d)
        z = jnp.dot(h, w_ref[:, c0:c0 + nc], preferred_element_type=F32)
        cols = slice(p0, p0 + nc)
        if part == 0:
            q_ref[0, :, cols] = _silu(z).astype(BF16)
        elif part in (1, 2):
            lg = lbl_ref[part - 1, :, cols]
            e = jnp.exp(lg - jnp.max(lg, axis=0, keepdims=True))
            lb = jnp.sum(e[:layer_j + 1], axis=0, keepdims=True) / jnp.sum(e, axis=0, keepdims=True)
            f = lb + (1.0 - lb) * _sigmoid(z)
            (lff_ref if part == 1 else lfb_ref)[0, :, cols] = jnp.log(f)
        elif part == 3:
            v_ref[0, :, cols] = z.astype(BF16)
        else:
            gs_ref[0, :, cols] = _silu(z).astype(BF16)


def _hg_in(x, mods, layer, norm_g, w_in, lb_logits, layer_j, ctx, n_lat):
    b, t, d = x.shape
    tm = _row_tile(t)
    tile = pl.BlockSpec((1, tm, d), lambda bi, i: (bi, i, 0))
    sds = lambda dt: jax.ShapeDtypeStruct((b, t, d), dt)
    return pl.pallas_call(
        functools.partial(_hg_in_kernel, ctx=ctx, n_lat=n_lat, layer_j=layer_j),
        out_shape=(sds(BF16), sds(F32), sds(F32), sds(BF16), sds(BF16)),
        grid=(b, t // tm),
        in_specs=[tile, _resident((1, d)), _mod_spec(layer, 0, d), _mod_spec(layer, 1, d),
                  _resident(w_in.shape), _resident(lb_logits.shape)],
        out_specs=(tile, tile, tile, tile, tile),
        compiler_params=_cparams("parallel", "parallel"),
        name="hg_in",
    )(x, norm_g.reshape(1, d), mods, mods, w_in, lb_logits)


HG_PAIR = 2 * HG_DK
HG_SCAN_BLOCK = 256


def _split2(x):
    hi = x.astype(BF16)
    return hi, (x - hi.astype(F32)).astype(BF16)


def _hg_scan_kernel(qf_ref, lf_ref, vf_ref, qb_ref, lb_ref, vb_ref, s0_ref, of_ref, ob_ref, sfin_ref,
                    st_ref, st16_ref, qe_ref, qs_ref, kebd_ref, kdbd_ref, vbd_ref, dec_ref):
    i = pl.program_id(1)
    tb, d = qf_ref.shape[1], qf_ref.shape[2]
    n_chunks = tb // CHUNK
    n_pairs = d // HG_PAIR

    @pl.when(i == 0)
    def _():
        st_ref[...] = s0_ref[0]
        for dn in range(2):
            for p in range(n_pairs):
                st16_ref[dn, p] = s0_ref[0, dn, p].T.astype(BF16)
        kebd_ref[...] = jnp.zeros(kebd_ref.shape, BF16)
        kdbd_ref[...] = jnp.zeros(kdbd_ref.shape, BF16)
        vbd_ref[...] = jnp.zeros(vbd_ref.shape, BF16)

    row = lax.broadcasted_iota(jnp.int32, (tb, tb), 0)
    col = lax.broadcasted_iota(jnp.int32, (tb, tb), 1)
    same_chunk = (row // CHUNK) == (col // CHUNK)
    r_in = lax.broadcasted_iota(jnp.int32, (CHUNK, 2 * CHUNK), 0)
    c_in = lax.broadcasted_iota(jnp.int32, (CHUNK, 2 * CHUNK), 1) % CHUNK
    dirs = (
        (qf_ref, lf_ref, vf_ref, of_ref, same_chunk & (row >= col), r_in >= c_in, CHUNK // 2, CHUNK - 1),
        (qb_ref, lb_ref, vb_ref, ob_ref, same_chunk & (row <= col), r_in <= c_in, CHUNK // 2 - 1, 0),
    )

    for dn, (q_ref, lfd_ref, v_ref, _, keep_blk, _, ref_row, last_row) in enumerate(dirs):
        tri = jnp.where(keep_blk, 1.0, 0.0).astype(BF16)
        lf = lfd_ref[0]
        b = sum(jnp.dot(tri, part, preferred_element_type=F32) for part in _split2(lf))
        for c in range(n_chunks):
            rows = slice(c * CHUNK, (c + 1) * CHUNK)
            bc = b[rows]
            b_mid = bc[ref_row:ref_row + 1, :]
            b_end = bc[last_row:last_row + 1, :]
            e_q = jnp.exp(bc - b_mid)
            qe = q_ref[0, rows, :].astype(F32) * e_q
            ke = (1.0 - jnp.exp(lf[rows])) * (1.0 / e_q)
            qe_ref[dn, rows, :] = qe.astype(BF16)
            qs_ref[dn, rows, :] = (qe * jnp.exp(b_mid)).astype(BF16)
            ke16 = ke.astype(BF16)
            kd16 = (ke * jnp.exp(b_end - b_mid)).astype(BF16)
            for h in range(2 * n_pairs):
                p, hh = divmod(h, 2)
                cols = slice(h * HG_DK, (h + 1) * HG_DK)
                blk = (slice(hh * CHUNK, (hh + 1) * CHUNK), slice(hh * HG_DK, (hh + 1) * HG_DK))
                kebd_ref[dn, c, p, blk[0], blk[1]] = ke16[:, cols]
                kdbd_ref[dn, c, p, blk[0], blk[1]] = kd16[:, cols]
                vbd_ref[dn, c, p, blk[0], blk[1]] = v_ref[0, rows, cols]
            dec_ref[dn, c:c + 1, :] = jnp.exp(b_end)

    for c in range(n_chunks):
        for dn, (_, _, v_ref, o_ref, _, keep, _, _) in enumerate(dirs):
            cc = c if dn == 0 else n_chunks - 1 - c
            rows = slice(cc * CHUNK, (cc + 1) * CHUNK)
            for p in range(n_pairs):
                pc = slice(p * HG_PAIR, (p + 1) * HG_PAIR)
                scores = lax.dot_general(qe_ref[dn, rows, pc], kebd_ref[dn, cc, p], (((1,), (1,)), ((), ())),
                                         preferred_element_type=F32)
                scores = jnp.where(keep, scores, 0.0).astype(BF16)
                o = jnp.dot(scores, vbd_ref[dn, cc, p], preferred_element_type=F32)
                o = o + jnp.dot(qs_ref[dn, rows, pc], st16_ref[dn, p], preferred_element_type=F32)
                o_ref[0, rows, pc] = o.astype(BF16)
                v_stack = jnp.concatenate(
                    [v_ref[0, rows, p * HG_PAIR:p * HG_PAIR + HG_DK],
                     v_ref[0, rows, p * HG_PAIR + HG_DK:(p + 1) * HG_PAIR]], axis=0)
                kv = lax.dot_general(v_stack, kdbd_ref[dn, cc, p], (((0,), (0,)), ((), ())),
                                     preferred_element_type=F32)
                for hh in range(2):
                    sl = slice(hh * HG_DK, (hh + 1) * HG_DK)
                    cols = slice(p * HG_PAIR + hh * HG_DK, p * HG_PAIR + (hh + 1) * HG_DK)
                    new = st_ref[dn, p, sl, sl] * dec_ref[dn, cc:cc + 1, cols] + kv[:, sl]
                    st_ref[dn, p, sl, sl] = new
                    st16_ref[dn, p, sl, sl] = new.T.astype(BF16)

    @pl.when(i == pl.num_programs(1) - 1)
    def _():
        sfin_ref[0] = st_ref[...]


def _hg_scan(q, lf_f, lf_b, v, s0):
    b, t, d = q.shape
    tb = HG_SCAN_BLOCK
    nblk = t // tb
    fwd = pl.BlockSpec((1, tb, d), lambda bi, i: (bi, i, 0))
    bwd = pl.BlockSpec((1, tb, d), lambda bi, i: (bi, nblk - 1 - i, 0))
    st_spec = pl.BlockSpec((1,) + s0.shape[1:], lambda bi, i: (bi, 0, 0, 0, 0))
    blockdiag = pltpu.VMEM((2, tb // CHUNK, d // HG_PAIR, 2 * CHUNK, HG_PAIR), BF16)
    return pl.pallas_call(
        _hg_scan_kernel,
        out_shape=(jax.ShapeDtypeStruct((b, t, d), BF16), jax.ShapeDtypeStruct((b, t, d), BF16),
                   jax.ShapeDtypeStruct(s0.shape, F32)),
        grid=(b, nblk),
        in_specs=[fwd, fwd, fwd, bwd, bwd, bwd, st_spec],
        out_specs=(fwd, bwd, st_spec),
        scratch_shapes=[pltpu.VMEM(s0.shape[1:], F32), pltpu.VMEM(s0.shape[1:], BF16),
                        pltpu.VMEM((2, tb, d), BF16), pltpu.VMEM((2, tb, d), BF16),
                        blockdiag, blockdiag, blockdiag, pltpu.VMEM((2, tb // CHUNK, d), F32)],
        compiler_params=_cparams("parallel", "arbitrary"),
        name="hg_scan",
    )(q, lf_f, v, q, lf_b, v, s0)


def _lru_in_kernel(x_ref, g_ref, sh_ref, sc_ref, w_ref, gg_ref, xr_ref, *, ctx, n_lat):
    d = x_ref.shape[-1]
    sh = _mod_row(sh_ref, ctx, n_lat)
    sc = _mod_row(sc_ref, ctx, n_lat)
    h = (_rms(x_ref[0], g_ref[...]) * (1.0 + sc) + sh).astype(BF16)
    nc = 2 * MXU_N
    for c0 in range(0, 2 * d, nc):
        part, p0 = divmod(c0, d)
        z = jnp.dot(h, w_ref[:, c0:c0 + nc], preferred_element_type=F32)
        if part == 0:
            gg_ref[0, :, p0:p0 + nc] = _gelu_tanh(z).astype(BF16)
        else:
            xr_ref[0, :, p0:p0 + nc] = z


def _lru_in(x, mods, layer, norm_g, w_in, ctx, n_lat):
    b, t, d = x.shape
    tm = _row_tile(t)
    tile = pl.BlockSpec((1, tm, d), lambda bi, i: (bi, i, 0))
    return pl.pallas_call(
        functools.partial(_lru_in_kernel, ctx=ctx, n_lat=n_lat),
        out_shape=(jax.ShapeDtypeStruct((b, t, d), BF16), jax.ShapeDtypeStruct((b, t, d), F32)),
        grid=(b, t // tm),
        in_specs=[tile, _resident((1, d)), _mod_spec(layer, 0, d), _mod_spec(layer, 1, d),
                  _resident(w_in.shape)],
        out_specs=(tile, tile),
        compiler_params=_cparams("parallel", "parallel"),
        name="lru_in",
    )(x, norm_g.reshape(1, d), mods, mods, w_in)


LRU_COL_BLOCK = BF16_ROWS
LRU_GATE_ROWS = MXU_N
LRU_SCAN_SEGS = 2
LOG2_E = 1.4426950408889634
LN_2 = 0.6931471805599453
HALO = SUBLANES


def _softplus(x):
    y = jnp.exp(-jnp.abs(x))
    u = 1.0 + y
    log1p = jnp.where(u == 1.0, y, jnp.log(u) * (y / (u - 1.0)))
    return jnp.maximum(x, 0.0) + log1p


def _lru_scan_kernel(xf_ref, xfp_ref, xfn_ref, xb_ref, xbp_ref, xbn_ref, cw_ref, cb_ref, wa_ref, wx_ref,
                     ba_ref, bx_ref, lam_ref, h0_ref, hf_ref, hb_ref, hfin_ref, xpad_ref, a_ref, u_ref,
                     hl_ref, pr_ref, carry_ref):
    j = pl.program_id(2)
    nj = pl.num_programs(2)
    _, nr, wb, cb = xf_ref.shape

    @pl.when(j == 0)
    def _():
        carry_ref[...] = h0_ref[:, 0]

    col = lax.broadcasted_iota(jnp.int32, (wb, cb), 0)
    left = LRU_CONV_W - 1 - (LRU_CONV_W - 1) // 2
    right = LRU_CONV_W - 1 - left
    dirs = ((xf_ref, xfp_ref, xfn_ref, j), (xb_ref, xbp_ref, xbn_ref, nj - 1 - j))
    for dn, (x_ref, xp_ref, xn_ref, jb) in enumerate(dirs):
        for k in range(left):
            edge = jnp.where(jb > 0, xp_ref[0, HALO - left + k, wb - 1:wb, :], 0.0)
            xpad_ref[dn, k] = jnp.where(col == 0, edge, pltpu.roll(x_ref[0, nr - left + k], 1, axis=0))
        xpad_ref[dn, left:left + nr] = x_ref[0]
        for k in range(right):
            edge = jnp.where(jb < nj - 1, xn_ref[0, k, 0:1, :], 0.0)
            xpad_ref[dn, left + nr + k] = jnp.where(col == wb - 1, edge,
                                                    pltpu.roll(x_ref[0, k], wb - 1, axis=0))

    cw = [cw_ref[k:k + 1, :] for k in range(LRU_CONV_W)]
    cbias = cb_ref[...]
    k2 = [(-0.5 * LRU_C * LOG2_E) * _softplus(-lam_ref[dn]) for dn in range(2)]
    rc = LRU_GATE_ROWS // wb

    def gate_chunk(ci, carry):
        r0 = pl.multiple_of(ci * rc, rc)
        for dn in range(2):
            xc = cbias + sum(cw[k] * xpad_ref[dn, pl.ds(r0 + k, rc)] for k in range(LRU_CONV_W))
            xc = xc.reshape(rc * wb, cb)
            xcb = xc.astype(BF16)
            tr = jnp.tanh(0.5 * (jnp.dot(xcb, wa_ref[dn, 0], preferred_element_type=F32) + ba_ref[dn]))
            ti = jnp.tanh(0.5 * (jnp.dot(xcb, wx_ref[dn, 0], preferred_element_type=F32) + bx_ref[dn]))
            log2_a = k2[dn] * tr + k2[dn]
            a = jnp.exp2(log2_a)
            m2 = (a * a + 1.0) * jnp.tanh(log2_a * (-LN_2))
            mult = jnp.where(m2 > 0.0, m2 * lax.rsqrt(m2), 0.0)
            hx = 0.5 * xc
            a_ref[dn, pl.ds(r0, rc)] = a.reshape(rc, wb, cb)
            u_ref[dn, pl.ds(r0, rc)] = (mult * (hx * ti + hx)).reshape(rc, wb, cb)
        return carry

    lax.fori_loop(0, nr // rc, gate_chunk, 0)

    n_seg = LRU_SCAN_SEGS
    seg = nr // n_seg

    def scan_step(rr, carry):
        out = []
        for dn in range(2):
            for s in range(n_seg):
                h, p = carry[dn * n_seg + s]
                r = s * seg + (rr if dn == 0 else seg - 1 - rr)
                a = a_ref[dn, r]
                h = a * h + u_ref[dn, r]
                p = a * p
                hl_ref[dn, r] = h
                pr_ref[dn, r] = p
                out.append((h, p))
        return tuple(out)

    zero = jnp.zeros((wb, cb), F32)
    one = jnp.ones((wb, cb), F32)
    ends = lax.fori_loop(0, seg, scan_step, ((zero, one),) * (2 * n_seg), unroll=2)

    for dn, o_ref in enumerate((hf_ref, hb_ref)):
        c_in = carry_ref[dn]
        c_units = [zero] * n_seg
        wls = range(wb) if dn == 0 else range(wb - 1, -1, -1)
        segs = range(n_seg) if dn == 0 else range(n_seg - 1, -1, -1)
        for wl in wls:
            for s in segs:
                h_end, p_end = ends[dn * n_seg + s]
                c_units[s] = jnp.where(col == wl, c_in, c_units[s])
                c_in = p_end[wl:wl + 1, :] * c_in + h_end[wl:wl + 1, :]
        carry_ref[dn] = c_in
        for s in range(n_seg):
            rows = slice(s * seg, (s + 1) * seg)
            o_ref[0, rows] = (hl_ref[dn, rows] + pr_ref[dn, rows] * c_units[s]).astype(o_ref.dtype)

    @pl.when(j == nj - 1)
    def _():
        hfin_ref[:, 0] = carry_ref[...]


def _lru_scan(xr, conv_w, conv_b, wa, wx, ba, bx, lam, h0, grid_w):
    b, t, c = xr.shape
    nr = t // grid_w
    cb = c // LRU_BLOCKS
    wb = LRU_COL_BLOCK
    nj = grid_w // wb
    x4 = xr.reshape(b, nr, grid_w, c)
    up = lambda j: j
    down = lambda j: nj - 1 - j
    main = lambda cj: pl.BlockSpec((1, nr, wb, cb), lambda bi, n, j: (bi, 0, cj(j), n))
    prev = lambda cj: pl.BlockSpec((1, HALO, wb, cb),
                                   lambda bi, n, j: (bi, nr // HALO - 1, jnp.maximum(cj(j) - 1, 0), n))
    nxt = lambda cj: pl.BlockSpec((1, HALO, wb, cb),
                                  lambda bi, n, j: (bi, 0, jnp.minimum(cj(j) + 1, nj - 1), n))
    chan = lambda rows: pl.BlockSpec((rows, cb), lambda bi, n, j: (0, n))
    per_dir = pl.BlockSpec((2, 1, cb), lambda bi, n, j: (0, 0, n))
    gate_w = pl.BlockSpec((2, 1, cb, cb), lambda bi, n, j: (0, n, 0, 0))
    state = pl.BlockSpec((2, 1, 1, cb), lambda bi, n, j: (0, bi, 0, n))
    hf, hb, hfin = pl.pallas_call(
        _lru_scan_kernel,
        out_shape=(jax.ShapeDtypeStruct(x4.shape, BF16), jax.ShapeDtypeStruct(x4.shape, BF16),
                   jax.ShapeDtypeStruct(h0.shape, F32)),
        grid=(b, LRU_BLOCKS, nj),
        in_specs=[main(up), prev(up), nxt(up), main(down), prev(down), nxt(down),
                  chan(LRU_CONV_W), chan(1), gate_w, gate_w, per_dir, per_dir, per_dir, state],
        out_specs=(main(up), main(down), state),
        scratch_shapes=[pltpu.VMEM((2, nr + LRU_CONV_W - 1, wb, cb), F32)]
        + [pltpu.VMEM((2, nr, wb, cb), F32)] * 4 + [pltpu.VMEM((2, 1, cb), F32)],
        compiler_params=_cparams("parallel", "parallel", "arbitrary"),
        name="lru_scan",
    )(x4, x4, x4, x4, x4, x4, conv_w, conv_b.reshape(1, c), wa, wx,
      ba.reshape(2, 1, c), bx.reshape(2, 1, c), lam.reshape(2, 1, c), h0)
    return hf.reshape(b, t, c), hb.reshape(b, t, c), hfin


FFN_ROW_TILE = 512


def _mix_ffn_kernel(af_ref, ab_ref, gs_ref, xm_ref, afp_ref, abp_ref, gsp_ref, xp_ref, afn_ref, abn_ref,
                    gsn_ref, xn_ref, gain_ref, wo_ref, gate1_ref, g_ref, sh_ref, sc_ref, gate_ref, wup_ref,
                    cw_ref, cb_ref, wdn_ref, fg_ref, out_ref, y_ref, lat_ref, h_ref, act_ref,
                    *, head_norm, ctx, n_lat, final_norm):
    i = pl.program_id(1)
    tm, d = xm_ref.shape[1], xm_ref.shape[2]
    dff = wdn_ref.shape[0]
    lo = slice(BF16_ROWS - HALO, BF16_ROWS)
    hi = slice(0, HALO)

    def readout_in(af, ab, gs):
        o = af.astype(F32) + ab.astype(F32)
        if head_norm:
            dv = gain_ref.shape[-1]
            o = jnp.concatenate([_rms(o[:, c0:c0 + dv], gain_ref[...]) for c0 in range(0, d, dv)], axis=-1)
        return o * gs.astype(F32)

    y_ref[...] = jnp.concatenate(
        [readout_in(afp_ref[0, lo], abp_ref[0, lo], gsp_ref[0, lo]),
         readout_in(af_ref[0], ab_ref[0], gs_ref[0]),
         readout_in(afn_ref[0, hi], abn_ref[0, hi], gsn_ref[0, hi])], axis=0).astype(BF16)
    x_all = jnp.concatenate([xp_ref[0, lo], xm_ref[0], xn_ref[0, hi]], axis=0)
    lat_ref[...] = x_all + _mod_row(gate1_ref, ctx, n_lat) * jnp.dot(y_ref[...], wo_ref[...],
                                                                     preferred_element_type=F32)

    g = g_ref[...]
    sh = _mod_row(sh_ref, ctx, n_lat)
    sc = _mod_row(sc_ref, ctx, n_lat)
    mod = lambda x: _rms(x, g) * (1.0 + sc) + sh
    hp = jnp.where(i > 0, mod(lat_ref[0:HALO]), 0.0)
    hn = jnp.where(i < pl.num_programs(1) - 1, mod(lat_ref[HALO + tm:]), 0.0)
    h_ref[...] = jnp.concatenate([hp, mod(lat_ref[HALO:HALO + tm]), hn], axis=0).astype(BF16)
    rows = tm + 2 * HALO
    nc = MXU_N
    for c0 in range(0, dff, nc):
        halves = []
        for base in (c0, dff + c0):
            u = jnp.dot(h_ref[...], wup_ref[:, base:base + nc], preferred_element_type=F32)
            cols = slice(base, base + nc)
            conv = (cb_ref[:, cols]
                    + cw_ref[0:1, cols] * pltpu.roll(u, 1, axis=0)[HALO:HALO + tm]
                    + cw_ref[1:2, cols] * u[HALO:HALO + tm]
                    + cw_ref[2:3, cols] * pltpu.roll(u, rows - 1, axis=0)[HALO:HALO + tm])
            halves.append(conv)
        act_ref[:, c0:c0 + nc] = (_silu(halves[0]) * halves[1]).astype(BF16)
    z = jnp.dot(act_ref[...], wdn_ref[...], preferred_element_type=F32)
    y = lat_ref[HALO:HALO + tm] + _mod_row(gate_ref, ctx, n_lat) * z
    out_ref[0] = _rms(y, fg_ref[...]) if final_norm else y


def _mix_ffn(a_f, a_b, gs, x, mods, layer, head_gain, w_out, norm_g, w_up, conv_w, conv_b, w_down, final_g,
             ctx, n_lat, final_norm):
    b, t, d = x.shape
    dff = w_down.shape[0]
    tm = _row_tile(t, FFN_ROW_TILE)
    per = tm // BF16_ROWS
    tile = pl.BlockSpec((1, tm, d), lambda bi, i: (bi, i, 0))
    prev = pl.BlockSpec((1, BF16_ROWS, d), lambda bi, i: (bi, jnp.maximum(i * per - 1, 0), 0))
    nxt = pl.BlockSpec((1, BF16_ROWS, d), lambda bi, i: (bi, jnp.minimum((i + 1) * per, t // BF16_ROWS - 1), 0))
    head_norm = head_gain is not None
    gain = head_gain.reshape(1, -1) if head_norm else jnp.ones((1, LANES), F32)
    rows = tm + 2 * HALO
    return pl.pallas_call(
        functools.partial(_mix_ffn_kernel, head_norm=head_norm, ctx=ctx, n_lat=n_lat, final_norm=final_norm),
        out_shape=jax.ShapeDtypeStruct((b, t, d), F32),
        grid=(b, t // tm),
        in_specs=[tile, tile, tile, tile, prev, prev, prev, prev, nxt, nxt, nxt, nxt,
                  _resident(gain.shape), _resident(w_out.shape), _mod_spec(layer, 2, d),
                  _resident((1, d)), _mod_spec(layer, 3, d), _mod_spec(layer, 4, d), _mod_spec(layer, 5, d),
                  _resident(w_up.shape), _resident(conv_w.shape), _resident((1, 2 * dff)),
                  _resident(w_down.shape), _resident((1, d))],
        out_specs=tile,
        scratch_shapes=[pltpu.VMEM((rows, d), BF16), pltpu.VMEM((rows, d), F32), pltpu.VMEM((rows, d), BF16),
                        pltpu.VMEM((tm, dff), BF16)],
        compiler_params=_cparams("parallel", "parallel"),
        name="mix_ffn",
    )(a_f, a_b, gs, x, a_f, a_b, gs, x, a_f, a_b, gs, x, gain, w_out, mods,
      norm_g.reshape(1, d), mods, mods, mods, w_up, conv_w, conv_b.reshape(1, 2 * dff), w_down,
      final_g.reshape(1, d))


def kernel(x, c, ctx, c_ctx, w_ada, b_ada, norm_g, hg_w_in, hg_lb_logits, hg_gnorm, hg_w_out, lru_w_in,
           lru_conv_w, lru_conv_b, lru_wa, lru_ba, lru_wx, lru_bx, lru_lambda, lru_w_out, ffn_w_up,
           ffn_conv_w, ffn_conv_b, ffn_w_down, final_g):
    nb, _, d = x.shape
    depth = w_ada.shape[0]
    n_mixers = 2
    bf = lambda w: w.astype(BF16)

    cond = jnp.concatenate([c, c_ctx[None, :], jnp.zeros((COND_ROWS - nb - 1, d), F32)], axis=0)
    mods = _ada(cond.T, w_ada, b_ada, nb + 1)

    lat, cx = x, ctx
    for l in range(depth):
        last = l == depth - 1
        j = l // n_mixers
        if l % n_mixers == 0:
            w_in, w_out = bf(hg_w_in[j]), bf(hg_w_out[j])
            s0 = jnp.zeros((nb, 2, d // HG_PAIR, HG_PAIR, HG_PAIR), F32)
            lb_logits = jnp.swapaxes(hg_lb_logits, 0, 1)
            pc = _hg_in(cx, mods, l, norm_g[l, 0], w_in, lb_logits, j, True, nb)
            oc_f, oc_b, s_ctx = _hg_scan(pc[0], pc[1], pc[2], pc[3], s0)
            pl_ = _hg_in(lat, mods, l, norm_g[l, 0], w_in, lb_logits, j, False, nb)
            ol_f, ol_b, _ = _hg_scan(pl_[0], pl_[1], pl_[2], pl_[3], s_ctx)
            mix_lat = (ol_f, ol_b, pl_[4])
            mix_ctx = (oc_f, oc_b, pc[4])
            head_gain = hg_gnorm[j]
        else:
            w_in, w_out = bf(lru_w_in[j]), bf(lru_w_out[j])
            wa, wx = bf(lru_wa[j]), bf(lru_wx[j])
            scan = functools.partial(_lru_scan, conv_w=lru_conv_w[j], conv_b=lru_conv_b[j], wa=wa, wx=wx,
                                     ba=lru_ba[j], bx=lru_bx[j], lam=lru_lambda[j])
            wc = LRU_COL_BLOCK
            t_ctx = cx.shape[1]
            flip = lambda a: jnp.swapaxes(a.reshape(nb, wc, t_ctx // wc, d), 1, 2).reshape(nb, t_ctx, d)
            unflip = lambda a: jnp.swapaxes(a.reshape(nb, t_ctx // wc, wc, d), 1, 2).reshape(nb, t_ctx, d)
            gg_c, xr_c = _lru_in(cx, mods, l, norm_g[l, 0], w_in, True, nb)
            hc_f, hc_b, h_ctx = scan(flip(xr_c), h0=jnp.zeros((2, nb, 1, d), F32), grid_w=wc)
            gg_l, xr_l = _lru_in(lat, mods, l, norm_g[l, 0], w_in, False, nb)
            hl_f, hl_b, _ = scan(xr_l, h0=h_ctx, grid_w=GRID_W)
            mix_lat = (hl_f, hl_b, gg_l)
            mix_ctx = (hc_f, hc_b, gg_c)
            head_gain = None
        mix_ffn = functools.partial(_mix_ffn, mods=mods, layer=l, head_gain=head_gain, w_out=w_out,
                                    norm_g=norm_g[l, 1], w_up=bf(ffn_w_up[l]), conv_w=ffn_conv_w[l],
                                    conv_b=ffn_conv_b[l], w_down=bf(ffn_w_down[l]), final_g=final_g, n_lat=nb)
        lat = mix_ffn(*mix_lat, lat, ctx=False, final_norm=last)
        if not last:
            if head_gain is None:
                mix_ctx = (unflip(mix_ctx[0]), unflip(mix_ctx[1]), mix_ctx[2])
            cx = mix_ffn(*mix_ctx, cx, ctx=True, final_norm=False)
    return lat
```

```python
import functools

import jax
import jax.numpy as jnp
from jax import lax
from jax.experimental import pallas as pl
from jax.experimental.pallas import tpu as pltpu

F32 = jnp.float32
BF16 = jnp.bfloat16

EPS = 1e-6
GRID_W = 64
HG_DK = 128
CHUNK = 64
LRU_BLOCKS = 4
LRU_C = 8.0
LRU_CONV_W = 4
FFN_CONV_W = 3
N_MOD = 6

SUBLANES = 8
LANES = 128
BF16_ROWS = 16
MXU_N = 256
VMEM_LIMIT_BYTES = 56 * 1024 * 1024
COND_ROWS = SUBLANES


def _cparams(*sem):
    return pltpu.CompilerParams(dimension_semantics=sem, vmem_limit_bytes=VMEM_LIMIT_BYTES)


def _resident(shape):
    zeros = (0,) * len(shape)
    return pl.BlockSpec(shape, lambda *_: zeros, pipeline_mode=pl.Buffered(1))


IN_PROJ_ROW_TILE = 1024


def _row_tile(t, largest):
    tm = largest
    while t % tm:
        tm //= 2
    return tm


def _sigmoid(x):
    return 0.5 + 0.5 * jnp.tanh(0.5 * x)


def _silu(x):
    return x * _sigmoid(x)


def _gelu_tanh(x):
    return 0.5 * x * (1.0 + jnp.tanh(0.7978845608028654 * (x + 0.044715 * (x * x * x))))


def _rms(x, g):
    return x * lax.rsqrt(jnp.mean(x * x, axis=-1, keepdims=True) + EPS) * g


def _mod_row(mod_ref, ctx, n_lat):
    if ctx:
        return mod_ref[0, n_lat:n_lat + 1, :]
    return mod_ref[0, pl.ds(pl.program_id(0), 1), :]


def _mod_spec(layer, k, d):
    return pl.BlockSpec((1, COND_ROWS, d), lambda *_: (layer, 0, k))


CAST_BLOCK_BYTES = 4 * 1024 * 1024


def _cast_kernel(w_ref, o_ref, *, scale):
    w = w_ref[0]
    o_ref[...] = (w if scale == 1.0 else w * scale).astype(o_ref.dtype)


def _layer_bf16(w, layer, scale=1.0):
    shape = w.shape[1:]
    cols = shape[-1]
    rows = w[0].size // cols
    tr = rows
    while tr * cols * 4 > CAST_BLOCK_BYTES and tr % (2 * BF16_ROWS) == 0:
        tr //= 2
    out = pl.pallas_call(
        functools.partial(_cast_kernel, scale=scale),
        out_shape=jax.ShapeDtypeStruct((rows, cols), BF16),
        grid=(rows // tr,),
        in_specs=[pl.BlockSpec((1, tr, cols), lambda i: (layer, i, 0))],
        out_specs=pl.BlockSpec((tr, cols), lambda i: (i, 0)),
        compiler_params=_cparams("parallel"),
        name="cast_bf16",
    )(w.reshape(w.shape[0], rows, cols))
    return out.reshape(shape)


def _ada_kernel(ct_ref, w_ref, b_ref, o_ref, *, n_cond):
    ct = ct_ref[...]
    s = _silu(ct)
    w = w_ref[0]
    o_ref[0] = jnp.zeros(o_ref.shape[1:], F32) + b_ref[0]
    for m in range(n_cond):
        o_ref[0, m:m + 1, :] = jnp.sum(w * s[:, m:m + 1], axis=0, keepdims=True) + b_ref[0]


def _ada(cond_t, w_ada, b_ada, n_cond):
    depth, d, n = w_ada.shape
    tn = 1024
    return pl.pallas_call(
        functools.partial(_ada_kernel, n_cond=n_cond),
        out_shape=jax.ShapeDtypeStruct((depth, COND_ROWS, n), F32),
        grid=(depth, n // tn),
        in_specs=[
            pl.BlockSpec((d, COND_ROWS), lambda l, j: (0, 0)),
            pl.BlockSpec((1, d, tn), lambda l, j: (l, 0, j)),
            pl.BlockSpec((1, 1, tn), lambda l, j: (l, 0, j)),
        ],
        out_specs=pl.BlockSpec((1, COND_ROWS, tn), lambda l, j: (l, 0, j)),
        compiler_params=_cparams("parallel", "parallel"),
        name="ada",
    )(cond_t, w_ada, b_ada.reshape(depth, 1, n))


def _split2(x):
    hi = x.astype(BF16)
    return hi, (x - hi.astype(F32)).astype(BF16)


def _hg_in_kernel(x_ref, g_ref, sh_ref, sc_ref, w_ref, lbl_ref, q_ref, lff_ref, lfb_ref, v_ref, gs_ref,
                  *, ctx, n_lat, layer_j):
    d = x_ref.shape[-1]
    sh = _mod_row(sh_ref, ctx, n_lat)
    sc = _mod_row(sc_ref, ctx, n_lat)
    h = (_rms(x_ref[0], g_ref[...]) * (1.0 + sc) + sh).astype(BF16)
    nc = 2 * MXU_N
    for c0 in range(0, 5 * d, nc):
        part, p0 = divmod(c0, d)
        z = jnp.dot(h, w_ref[:, c0:c0 + nc], preferred_element_type=F32)
        cols = slice(p0, p0 + nc)
        if part == 0:
            q_ref[0, :, cols] = _silu(z).astype(BF16)
        elif part in (1, 2):
            lg = lbl_ref[part - 1, :, cols]
            e = jnp.exp(lg - jnp.max(lg, axis=0, keepdims=True))
            lb = jnp.sum(e[:layer_j + 1], axis=0, keepdims=True) / jnp.sum(e, axis=0, keepdims=True)
            f = lb + (1.0 - lb) * _sigmoid(z)
            (lff_ref if part == 1 else lfb_ref)[0, :, cols] = jnp.log(f)
        elif part == 3:
            v_ref[0, :, cols] = z.astype(BF16)
        else:
            gs_ref[0, :, cols] = _silu(z).astype(BF16)


def _hg_in(x, mods, layer, norm_g, w_in, lb_logits, layer_j, ctx, n_lat):
    b, t, d = x.shape
    tm = _row_tile(t, IN_PROJ_ROW_TILE)
    tile = pl.BlockSpec((1, tm, d), lambda bi, i: (bi, i, 0))
    sds = lambda dt: jax.ShapeDtypeStruct((b, t, d), dt)
    return pl.pallas_call(
        functools.partial(_hg_in_kernel, ctx=ctx, n_lat=n_lat, layer_j=layer_j),
        out_shape=(sds(BF16), sds(F32), sds(F32), sds(BF16), sds(BF16)),
        grid=(b, t // tm),
        in_specs=[tile, _resident((1, d)), _mod_spec(layer, 0, d), _mod_spec(layer, 1, d),
                  _resident(w_in.shape), _resident(lb_logits.shape)],
        out_specs=(tile, tile, tile, tile, tile),
        compiler_params=_cparams("parallel", "parallel"),
        name="hg_in",
    )(x, norm_g.reshape(1, d), mods, mods, w_in, lb_logits)


HG_PAIR = 2 * HG_DK
HG_SCAN_BLOCK = 256


def _hg_scan_kernel(qf_ref, lf_ref, vf_ref, qb_ref, lb_ref, vb_ref, s0_ref, of_ref, ob_ref, sfin_ref,
                    st_ref, st16_ref, qe_ref, qs_ref, kebd_ref, kdbd_ref, vbd_ref, dec_ref):
    i = pl.program_id(1)
    tb, d = qf_ref.shape[1], qf_ref.shape[2]
    n_chunks = tb // CHUNK
    n_pairs = d // HG_PAIR

    @pl.when(i == 0)
    def _():
        st_ref[...] = s0_ref[0]
        for dn in range(2):
            for p in range(n_pairs):
                st16_ref[dn, p] = s0_ref[0, dn, p].T.astype(BF16)
        kebd_ref[...] = jnp.zeros(kebd_ref.shape, BF16)
        kdbd_ref[...] = jnp.zeros(kdbd_ref.shape, BF16)
        vbd_ref[...] = jnp.zeros(vbd_ref.shape, BF16)

    row = lax.broadcasted_iota(jnp.int32, (tb, tb), 0)
    col = lax.broadcasted_iota(jnp.int32, (tb, tb), 1)
    same_chunk = (row // CHUNK) == (col // CHUNK)
    r_in = lax.broadcasted_iota(jnp.int32, (CHUNK, 2 * CHUNK), 0)
    c_in = lax.broadcasted_iota(jnp.int32, (CHUNK, 2 * CHUNK), 1) % CHUNK
    dirs = (
        (qf_ref, lf_ref, vf_ref, of_ref, same_chunk & (row >= col), r_in >= c_in, CHUNK // 2, CHUNK - 1),
        (qb_ref, lb_ref, vb_ref, ob_ref, same_chunk & (row <= col), r_in <= c_in, CHUNK // 2 - 1, 0),
    )

    for dn, (q_ref, lfd_ref, v_ref, _, keep_blk, _, ref_row, last_row) in enumerate(dirs):
        tri = jnp.where(keep_blk, 1.0, 0.0).astype(BF16)
        lf = lfd_ref[0]
        b = sum(jnp.dot(tri, part, preferred_element_type=F32) for part in _split2(lf))
        for c in range(n_chunks):
            rows = slice(c * CHUNK, (c + 1) * CHUNK)
            bc = b[rows]
            b_mid = bc[ref_row:ref_row + 1, :]
            b_end = bc[last_row:last_row + 1, :]
            e_q = jnp.exp(bc - b_mid)
            qe = q_ref[0, rows, :].astype(F32) * e_q
            ke = (1.0 - jnp.exp(lf[rows])) * (1.0 / e_q)
            qe_ref[dn, rows, :] = qe.astype(BF16)
            qs_ref[dn, rows, :] = (qe * jnp.exp(b_mid)).astype(BF16)
            ke16 = ke.astype(BF16)
            kd16 = (ke * jnp.exp(b_end - b_mid)).astype(BF16)
            for h in range(2 * n_pairs):
                p, hh = divmod(h, 2)
                cols = slice(h * HG_DK, (h + 1) * HG_DK)
                blk = (slice(hh * CHUNK, (hh + 1) * CHUNK), slice(hh * HG_DK, (hh + 1) * HG_DK))
                kebd_ref[dn, c, p, blk[0], blk[1]] = ke16[:, cols]
                kdbd_ref[dn, c, p, blk[0], blk[1]] = kd16[:, cols]
                vbd_ref[dn, c, p, blk[0], blk[1]] = v_ref[0, rows, cols]
            dec_ref[dn, c:c + 1, :] = jnp.exp(b_end)

    for c in range(n_chunks):
        for dn, (_, _, v_ref, o_ref, _, keep, _, _) in enumerate(dirs):
            cc = c if dn == 0 else n_chunks - 1 - c
            rows = slice(cc * CHUNK, (cc + 1) * CHUNK)
            for p in range(n_pairs):
                pc = slice(p * HG_PAIR, (p + 1) * HG_PAIR)
                scores = lax.dot_general(qe_ref[dn, rows, pc], kebd_ref[dn, cc, p], (((1,), (1,)), ((), ())),
                                         preferred_element_type=F32)
                scores = jnp.where(keep, scores, 0.0).astype(BF16)
                o = jnp.dot(scores, vbd_ref[dn, cc, p], preferred_element_type=F32)
                o = o + jnp.dot(qs_ref[dn, rows, pc], st16_ref[dn, p], preferred_element_type=F32)
                o_ref[0, rows, pc] = o.astype(BF16)
                v_stack = jnp.concatenate(
                    [v_ref[0, rows, p * HG_PAIR:p * HG_PAIR + HG_DK],
                     v_ref[0, rows, p * HG_PAIR + HG_DK:(p + 1) * HG_PAIR]], axis=0)
                kv = lax.dot_general(v_stack, kdbd_ref[dn, cc, p], (((0,), (0,)), ((), ())),
                                     preferred_element_type=F32)
                for hh in range(2):
                    sl = slice(hh * HG_DK, (hh + 1) * HG_DK)
                    cols = slice(p * HG_PAIR + hh * HG_DK, p * HG_PAIR + (hh + 1) * HG_DK)
                    new = st_ref[dn, p, sl, sl] * dec_ref[dn, cc:cc + 1, cols] + kv[:, sl]
                    st_ref[dn, p, sl, sl] = new
                    st16_ref[dn, p, sl, sl] = new.T.astype(BF16)

    @pl.when(i == pl.num_programs(1) - 1)
    def _():
        sfin_ref[0] = st_ref[...]


def _hg_scan(q, lf_f, lf_b, v, s0):
    b, t, d = q.shape
    tb = HG_SCAN_BLOCK
    nblk = t // tb
    fwd = pl.BlockSpec((1, tb, d), lambda bi, i: (bi, i, 0))
    bwd = pl.BlockSpec((1, tb, d), lambda bi, i: (bi, nblk - 1 - i, 0))
    st_spec = pl.BlockSpec((1,) + s0.shape[1:], lambda bi, i: (bi, 0, 0, 0, 0))
    blockdiag = pltpu.VMEM((2, tb // CHUNK, d // HG_PAIR, 2 * CHUNK, HG_PAIR), BF16)
    return pl.pallas_call(
        _hg_scan_kernel,
        out_shape=(jax.ShapeDtypeStruct((b, t, d), BF16), jax.ShapeDtypeStruct((b, t, d), BF16),
                   jax.ShapeDtypeStruct(s0.shape, F32)),
        grid=(b, nblk),
        in_specs=[fwd, fwd, fwd, bwd, bwd, bwd, st_spec],
        out_specs=(fwd, bwd, st_spec),
        scratch_shapes=[pltpu.VMEM(s0.shape[1:], F32), pltpu.VMEM(s0.shape[1:], BF16),
                        pltpu.VMEM((2, tb, d), BF16), pltpu.VMEM((2, tb, d), BF16),
                        blockdiag, blockdiag, blockdiag, pltpu.VMEM((2, tb // CHUNK, d), F32)],
        compiler_params=_cparams("parallel", "arbitrary"),
        name="hg_scan",
    )(q, lf_f, v, q, lf_b, v, s0)


def _lru_in_kernel(x_ref, g_ref, sh_ref, sc_ref, w_ref, gg_ref, xr_ref, *, ctx, n_lat):
    d = x_ref.shape[-1]
    sh = _mod_row(sh_ref, ctx, n_lat)
    sc = _mod_row(sc_ref, ctx, n_lat)
    h = (_rms(x_ref[0], g_ref[...]) * (1.0 + sc) + sh).astype(BF16)
    nc = 2 * MXU_N
    for c0 in range(0, 2 * d, nc):
        part, p0 = divmod(c0, d)
        z = jnp.dot(h, w_ref[:, c0:c0 + nc], preferred_element_type=F32)
        if part == 0:
            gg_ref[0, :, p0:p0 + nc] = _gelu_tanh(z).astype(BF16)
        else:
            xr_ref[0, :, p0:p0 + nc] = z


def _lru_in(x, mods, layer, norm_g, w_in, ctx, n_lat):
    b, t, d = x.shape
    tm = _row_tile(t, IN_PROJ_ROW_TILE)
    tile = pl.BlockSpec((1, tm, d), lambda bi, i: (bi, i, 0))
    return pl.pallas_call(
        functools.partial(_lru_in_kernel, ctx=ctx, n_lat=n_lat),
        out_shape=(jax.ShapeDtypeStruct((b, t, d), BF16), jax.ShapeDtypeStruct((b, t, d), F32)),
        grid=(b, t // tm),
        in_specs=[tile, _resident((1, d)), _mod_spec(layer, 0, d), _mod_spec(layer, 1, d),
                  _resident(w_in.shape)],
        out_specs=(tile, tile),
        compiler_params=_cparams("parallel", "parallel"),
        name="lru_in",
    )(x, norm_g.reshape(1, d), mods, mods, w_in)


LRU_COL_BLOCK = BF16_ROWS
LRU_GATE_ROWS = MXU_N
LRU_SCAN_SEGS = 2
LOG2_E = 1.4426950408889634
LN_2 = 0.6931471805599453
HALO = SUBLANES


def _softplus(x):
    y = jnp.exp(-jnp.abs(x))
    u = 1.0 + y
    log1p = jnp.where(u == 1.0, y, jnp.log(u) * (y / (u - 1.0)))
    return jnp.maximum(x, 0.0) + log1p


def _lru_scan_kernel(xf_ref, xfp_ref, xfn_ref, xb_ref, xbp_ref, xbn_ref, cw_ref, cb_ref, wa_ref, wx_ref,
                     ba_ref, bx_ref, lam_ref, h0_ref, hf_ref, hb_ref, hfin_ref, xpad_ref, a_ref, u_ref,
                     hl_ref, pr_ref, carry_ref):
    j = pl.program_id(2)
    nj = pl.num_programs(2)
    _, nr, wb, cb = xf_ref.shape

    @pl.when(j == 0)
    def _():
        carry_ref[...] = h0_ref[:, 0]

    col = lax.broadcasted_iota(jnp.int32, (wb, cb), 0)
    left = LRU_CONV_W - 1 - (LRU_CONV_W - 1) // 2
    right = LRU_CONV_W - 1 - left
    dirs = ((xf_ref, xfp_ref, xfn_ref, j), (xb_ref, xbp_ref, xbn_ref, nj - 1 - j))
    for dn, (x_ref, xp_ref, xn_ref, jb) in enumerate(dirs):
        for k in range(left):
            edge = jnp.where(jb > 0, xp_ref[0, HALO - left + k, wb - 1:wb, :], 0.0)
            xpad_ref[dn, k] = jnp.where(col == 0, edge, pltpu.roll(x_ref[0, nr - left + k], 1, axis=0))
        xpad_ref[dn, left:left + nr] = x_ref[0]
        for k in range(right):
            edge = jnp.where(jb < nj - 1, xn_ref[0, k, 0:1, :], 0.0)
            xpad_ref[dn, left + nr + k] = jnp.where(col == wb - 1, edge,
                                                    pltpu.roll(x_ref[0, k], wb - 1, axis=0))

    cw = [cw_ref[k:k + 1, :] for k in range(LRU_CONV_W)]
    cbias = cb_ref[...]
    k2 = [(-0.5 * LRU_C * LOG2_E) * _softplus(-lam_ref[dn]) for dn in range(2)]
    half_ba = [0.5 * ba_ref[dn] for dn in range(2)]
    half_bx = [0.5 * bx_ref[dn] for dn in range(2)]
    rc = LRU_GATE_ROWS // wb

    def gate_chunk(ci, carry):
        r0 = pl.multiple_of(ci * rc, rc)
        for dn in range(2):
            xc = cbias + sum(cw[k] * xpad_ref[dn, pl.ds(r0 + k, rc)] for k in range(LRU_CONV_W))
            xc = xc.reshape(rc * wb, cb)
            xcb = xc.astype(BF16)
            tr = jnp.tanh(jnp.dot(xcb, wa_ref[dn, 0], preferred_element_type=F32) + half_ba[dn])
            ti = jnp.tanh(jnp.dot(xcb, wx_ref[dn, 0], preferred_element_type=F32) + half_bx[dn])
            log2_a = k2[dn] * tr + k2[dn]
            a = jnp.exp2(log2_a)
            q2 = (a * a * 0.25 + 0.25) * jnp.tanh(log2_a * (-LN_2))
            half_mult = jnp.where(q2 > 0.0, q2 * lax.rsqrt(q2), 0.0)
            a_ref[dn, pl.ds(r0, rc)] = a.reshape(rc, wb, cb)
            u_ref[dn, pl.ds(r0, rc)] = (half_mult * (xc * ti + xc)).reshape(rc, wb, cb)
        return carry

    lax.fori_loop(0, nr // rc, gate_chunk, 0)

    n_seg = LRU_SCAN_SEGS
    seg = nr // n_seg

    def scan_step(rr, carry):
        out = []
        for dn in range(2):
            for s in range(n_seg):
                h, p = carry[dn * n_seg + s]
                r = s * seg + (rr if dn == 0 else seg - 1 - rr)
                a = a_ref[dn, r]
                h = a * h + u_ref[dn, r]
                p = a * p
                hl_ref[dn, r] = h
                pr_ref[dn, r] = p
                out.append((h, p))
        return tuple(out)

    zero = jnp.zeros((wb, cb), F32)
    one = jnp.ones((wb, cb), F32)
    ends = lax.fori_loop(0, seg, scan_step, ((zero, one),) * (2 * n_seg), unroll=2)

    for dn, o_ref in enumerate((hf_ref, hb_ref)):
        c_in = carry_ref[dn]
        c_units = [zero] * n_seg
        wls = range(wb) if dn == 0 else range(wb - 1, -1, -1)
        segs = range(n_seg) if dn == 0 else range(n_seg - 1, -1, -1)
        for wl in wls:
            for s in segs:
                h_end, p_end = ends[dn * n_seg + s]
                c_units[s] = jnp.where(col == wl, c_in, c_units[s])
                c_in = p_end[wl:wl + 1, :] * c_in + h_end[wl:wl + 1, :]
        carry_ref[dn] = c_in
        for s in range(n_seg):
            rows = slice(s * seg, (s + 1) * seg)
            o_ref[0, rows] = (hl_ref[dn, rows] + pr_ref[dn, rows] * c_units[s]).astype(o_ref.dtype)

    @pl.when(j == nj - 1)
    def _():
        hfin_ref[:, 0] = carry_ref[...]


def _lru_scan(xr, conv_w, conv_b, wa, wx, ba, bx, lam, h0, grid_w):
    b, t, c = xr.shape
    nr = t // grid_w
    cb = c // LRU_BLOCKS
    wb = LRU_COL_BLOCK
    nj = grid_w // wb
    x4 = xr.reshape(b, nr, grid_w, c)
    up = lambda j: j
    down = lambda j: nj - 1 - j
    main = lambda cj: pl.BlockSpec((1, nr, wb, cb), lambda bi, n, j: (bi, 0, cj(j), n))
    prev = lambda cj: pl.BlockSpec((1, HALO, wb, cb),
                                   lambda bi, n, j: (bi, nr // HALO - 1, jnp.maximum(cj(j) - 1, 0), n))
    nxt = lambda cj: pl.BlockSpec((1, HALO, wb, cb),
                                  lambda bi, n, j: (bi, 0, jnp.minimum(cj(j) + 1, nj - 1), n))
    chan = lambda rows: pl.BlockSpec((rows, cb), lambda bi, n, j: (0, n))
    per_dir = pl.BlockSpec((2, 1, cb), lambda bi, n, j: (0, 0, n))
    gate_w = pl.BlockSpec((2, 1, cb, cb), lambda bi, n, j: (0, n, 0, 0))
    state = pl.BlockSpec((2, 1, 1, cb), lambda bi, n, j: (0, bi, 0, n))
    hf, hb, hfin = pl.pallas_call(
        _lru_scan_kernel,
        out_shape=(jax.ShapeDtypeStruct(x4.shape, BF16), jax.ShapeDtypeStruct(x4.shape, BF16),
                   jax.ShapeDtypeStruct(h0.shape, F32)),
        grid=(b, LRU_BLOCKS, nj),
        in_specs=[main(up), prev(up), nxt(up), main(down), prev(down), nxt(down),
                  chan(LRU_CONV_W), chan(1), gate_w, gate_w, per_dir, per_dir, per_dir, state],
        out_specs=(main(up), main(down), state),
        scratch_shapes=[pltpu.VMEM((2, nr + LRU_CONV_W - 1, wb, cb), F32)]
        + [pltpu.VMEM((2, nr, wb, cb), F32)] * 4 + [pltpu.VMEM((2, 1, cb), F32)],
        compiler_params=_cparams("parallel", "parallel", "arbitrary"),
        name="lru_scan",
    )(x4, x4, x4, x4, x4, x4, conv_w, conv_b.reshape(1, c), wa, wx,
      ba.reshape(2, 1, c), bx.reshape(2, 1, c), lam.reshape(2, 1, c), h0)
    return hf.reshape(b, t, c), hb.reshape(b, t, c), hfin


FFN_ROW_TILE = 512


def _mix_ffn_kernel(af_ref, ab_ref, gs_ref, xm_ref, afp_ref, abp_ref, gsp_ref, xp_ref, afn_ref, abn_ref,
                    gsn_ref, xn_ref, gain_ref, wo_ref, gate1_ref, g_ref, sh_ref, sc_ref, gate_ref, wup_ref,
                    cw_ref, cb_ref, wdn_ref, fg_ref, out_ref, y_ref, lat_ref, h_ref, act_ref,
                    *, head_norm, ctx, n_lat, final_norm):
    i = pl.program_id(1)
    tm, d = xm_ref.shape[1], xm_ref.shape[2]
    dff = wdn_ref.shape[0]
    lo = slice(BF16_ROWS - HALO, BF16_ROWS)
    hi = slice(0, HALO)

    def readout_in(af, ab, gs):
        o = af.astype(F32) + ab.astype(F32)
        if head_norm:
            dv = gain_ref.shape[-1]
            o = jnp.concatenate([_rms(o[:, c0:c0 + dv], gain_ref[...]) for c0 in range(0, d, dv)], axis=-1)
        return o * gs.astype(F32)

    y_ref[...] = jnp.concatenate(
        [readout_in(afp_ref[0, lo], abp_ref[0, lo], gsp_ref[0, lo]),
         readout_in(af_ref[0], ab_ref[0], gs_ref[0]),
         readout_in(afn_ref[0, hi], abn_ref[0, hi], gsn_ref[0, hi])], axis=0).astype(BF16)
    x_all = jnp.concatenate([xp_ref[0, lo], xm_ref[0], xn_ref[0, hi]], axis=0)
    lat_ref[...] = x_all + _mod_row(gate1_ref, ctx, n_lat) * jnp.dot(y_ref[...], wo_ref[...],
                                                                     preferred_element_type=F32)

    g = g_ref[...]
    sh = _mod_row(sh_ref, ctx, n_lat)
    sc = _mod_row(sc_ref, ctx, n_lat)
    mod = lambda x: _rms(x, g) * (1.0 + sc) + sh
    hp = jnp.where(i > 0, mod(lat_ref[0:HALO]), 0.0)
    hn = jnp.where(i < pl.num_programs(1) - 1, mod(lat_ref[HALO + tm:]), 0.0)
    h_ref[...] = jnp.concatenate([hp, mod(lat_ref[HALO:HALO + tm]), hn], axis=0).astype(BF16)
    rows = tm + 2 * HALO
    nc = MXU_N
    for c0 in range(0, dff, nc):
        halves = []
        for base in (c0, dff + c0):
            u = jnp.dot(h_ref[...], wup_ref[:, base:base + nc], preferred_element_type=F32)
            cols = slice(base, base + nc)
            conv = (cb_ref[:, cols]
                    + cw_ref[0:1, cols] * pltpu.roll(u, 1, axis=0)[HALO:HALO + tm]
                    + cw_ref[1:2, cols] * u[HALO:HALO + tm]
                    + cw_ref[2:3, cols] * pltpu.roll(u, rows - 1, axis=0)[HALO:HALO + tm])
            halves.append(conv)
        act_ref[:, c0:c0 + nc] = (_silu(halves[0]) * halves[1]).astype(BF16)
    z = jnp.dot(act_ref[...], wdn_ref[...], preferred_element_type=F32)
    y = lat_ref[HALO:HALO + tm] + _mod_row(gate_ref, ctx, n_lat) * z
    out_ref[0] = _rms(y, fg_ref[...]) if final_norm else y


def _mix_ffn(a_f, a_b, gs, x, mods, layer, head_gain, w_out, norm_g, w_up, conv_w, conv_b, w_down, final_g,
             ctx, n_lat, final_norm):
    b, t, d = x.shape
    dff = w_down.shape[0]
    tm = _row_tile(t, FFN_ROW_TILE)
    per = tm // BF16_ROWS
    tile = pl.BlockSpec((1, tm, d), lambda bi, i: (bi, i, 0))
    prev = pl.BlockSpec((1, BF16_ROWS, d), lambda bi, i: (bi, jnp.maximum(i * per - 1, 0), 0))
    nxt = pl.BlockSpec((1, BF16_ROWS, d), lambda bi, i: (bi, jnp.minimum((i + 1) * per, t // BF16_ROWS - 1), 0))
    head_norm = head_gain is not None
    gain = head_gain.reshape(1, -1) if head_norm else jnp.ones((1, LANES), F32)
    rows = tm + 2 * HALO
    return pl.pallas_call(
        functools.partial(_mix_ffn_kernel, head_norm=head_norm, ctx=ctx, n_lat=n_lat, final_norm=final_norm),
        out_shape=jax.ShapeDtypeStruct((b, t, d), F32),
        grid=(b, t // tm),
        in_specs=[tile, tile, tile, tile, prev, prev, prev, prev, nxt, nxt, nxt, nxt,
                  _resident(gain.shape), _resident(w_out.shape), _mod_spec(layer, 2, d),
                  _resident((1, d)), _mod_spec(layer, 3, d), _mod_spec(layer, 4, d), _mod_spec(layer, 5, d),
                  _resident(w_up.shape), _resident(conv_w.shape), _resident((1, 2 * dff)),
                  _resident(w_down.shape), _resident((1, d))],
        out_specs=tile,
        scratch_shapes=[pltpu.VMEM((rows, d), BF16), pltpu.VMEM((rows, d), F32), pltpu.VMEM((rows, d), BF16),
                        pltpu.VMEM((tm, dff), BF16)],
        compiler_params=_cparams("parallel", "parallel"),
        name="mix_ffn",
    )(a_f, a_b, gs, x, a_f, a_b, gs, x, a_f, a_b, gs, x, gain, w_out, mods,
      norm_g.reshape(1, d), mods, mods, mods, w_up, conv_w, conv_b.reshape(1, 2 * dff), w_down,
      final_g.reshape(1, d))


def kernel(x, c, ctx, c_ctx, w_ada, b_ada, norm_g, hg_w_in, hg_lb_logits, hg_gnorm, hg_w_out, lru_w_in,
           lru_conv_w, lru_conv_b, lru_wa, lru_ba, lru_wx, lru_bx, lru_lambda, lru_w_out, ffn_w_up,
           ffn_conv_w, ffn_conv_b, ffn_w_down, final_g):
    nb, _, d = x.shape
    depth = w_ada.shape[0]
    n_mixers = 2
    bf = _layer_bf16

    cond = jnp.concatenate([c, c_ctx[None, :], jnp.zeros((COND_ROWS - nb - 1, d), F32)], axis=0)
    mods = _ada(cond.T, w_ada, b_ada, nb + 1)

    lat, cx = x, ctx
    for l in range(depth):
        last = l == depth - 1
        j = l // n_mixers
        if l % n_mixers == 0:
            w_in, w_out = bf(hg_w_in, j), bf(hg_w_out, j)
            s0 = jnp.zeros((nb, 2, d // HG_PAIR, HG_PAIR, HG_PAIR), F32)
            lb_logits = jnp.swapaxes(hg_lb_logits, 0, 1)
            pc = _hg_in(cx, mods, l, norm_g[l, 0], w_in, lb_logits, j, True, nb)
            oc_f, oc_b, s_ctx = _hg_scan(pc[0], pc[1], pc[2], pc[3], s0)
            pl_ = _hg_in(lat, mods, l, norm_g[l, 0], w_in, lb_logits, j, False, nb)
            ol_f, ol_b, _ = _hg_scan(pl_[0], pl_[1], pl_[2], pl_[3], s_ctx)
            mix_lat = (ol_f, ol_b, pl_[4])
            mix_ctx = (oc_f, oc_b, pc[4])
            head_gain = hg_gnorm[j]
        else:
            w_in, w_out = bf(lru_w_in, j), bf(lru_w_out, j)
            wa, wx = bf(lru_wa, j, 0.5), bf(lru_wx, j, 0.5)
            scan = functools.partial(_lru_scan, conv_w=lru_conv_w[j], conv_b=lru_conv_b[j], wa=wa, wx=wx,
                                     ba=lru_ba[j], bx=lru_bx[j], lam=lru_lambda[j])
            wc = LRU_COL_BLOCK
            t_ctx = cx.shape[1]
            flip = lambda a: jnp.swapaxes(a.reshape(nb, wc, t_ctx // wc, d), 1, 2).reshape(nb, t_ctx, d)
            unflip = lambda a: jnp.swapaxes(a.reshape(nb, t_ctx // wc, wc, d), 1, 2).reshape(nb, t_ctx, d)
            gg_c, xr_c = _lru_in(cx, mods, l, norm_g[l, 0], w_in, True, nb)
            hc_f, hc_b, h_ctx = scan(flip(xr_c), h0=jnp.zeros((2, nb, 1, d), F32), grid_w=wc)
            gg_l, xr_l = _lru_in(lat, mods, l, norm_g[l, 0], w_in, False, nb)
            hl_f, hl_b, _ = scan(xr_l, h0=h_ctx, grid_w=GRID_W)
            mix_lat = (hl_f, hl_b, gg_l)
            mix_ctx = (hc_f, hc_b, gg_c)
            head_gain = None
        mix_ffn = functools.partial(_mix_ffn, mods=mods, layer=l, head_gain=head_gain, w_out=w_out,
                                    norm_g=norm_g[l, 1], w_up=bf(ffn_w_up, l), conv_w=ffn_conv_w[l],
                                    conv_b=ffn_conv_b[l], w_down=bf(ffn_w_down, l), final_g=final_g, n_lat=nb)
        lat = mix_ffn(*mix_lat, lat, ctx=False, final_norm=last)
        if not last:
            if head_gain is None:
                mix_ctx = (unflip(mix_ctx[0]), unflip(mix_ctx[1]), mix_ctx[2])
            cx = mix_ffn(*mix_ctx, cx, ctx=True, final_norm=False)
    return lat
```

```python
import functools

import jax
import jax.numpy as jnp
from jax import lax
from jax.experimental import pallas as pl
from jax.experimental.pallas import tpu as pltpu

F32 = jnp.float32
BF16 = jnp.bfloat16

EPS = 1e-6
GRID_W = 64
HG_DK = 128
CHUNK = 64
LRU_BLOCKS = 4
LRU_C = 8.0
LRU_CONV_W = 4
FFN_CONV_W = 3
N_MOD = 6

SUBLANES = 8
LANES = 128
BF16_ROWS = 16
MXU_N = 256
VMEM_LIMIT_BYTES = 56 * 1024 * 1024
COND_ROWS = SUBLANES


def _cparams(*sem):
    return pltpu.CompilerParams(dimension_semantics=sem, vmem_limit_bytes=VMEM_LIMIT_BYTES)


def _resident(shape):
    zeros = (0,) * len(shape)
    return pl.BlockSpec(shape, lambda *_: zeros, pipeline_mode=pl.Buffered(1))


IN_PROJ_ROW_TILE = 1024


def _row_tile(t, largest):
    tm = largest
    while t % tm:
        tm //= 2
    return tm


def _sigmoid(x):
    return 0.5 + 0.5 * jnp.tanh(0.5 * x)


def _silu(x):
    return x * _sigmoid(x)


def _gelu_tanh(x):
    return 0.5 * x * (1.0 + jnp.tanh(0.7978845608028654 * (x + 0.044715 * (x * x * x))))


def _rms(x, g):
    return x * lax.rsqrt(jnp.mean(x * x, axis=-1, keepdims=True) + EPS) * g


def _mod_row(mod_ref, ctx, n_lat):
    if ctx:
        return mod_ref[0, n_lat:n_lat + 1, :]
    return mod_ref[0, pl.ds(pl.program_id(0), 1), :]


def _mod_spec(layer, k, d):
    return pl.BlockSpec((1, COND_ROWS, d), lambda *_: (layer, 0, k))


CAST_BLOCK_BYTES = 4 * 1024 * 1024


def _cast_kernel(w_ref, o_ref, *, scale):
    w = w_ref[0]
    o_ref[...] = (w if scale == 1.0 else w * scale).astype(o_ref.dtype)


def _layer_bf16(w, layer, scale=1.0):
    shape = w.shape[1:]
    cols = shape[-1]
    rows = w[0].size // cols
    tr = rows
    while tr * cols * 4 > CAST_BLOCK_BYTES and tr % (2 * BF16_ROWS) == 0:
        tr //= 2
    out = pl.pallas_call(
        functools.partial(_cast_kernel, scale=scale),
        out_shape=jax.ShapeDtypeStruct((rows, cols), BF16),
        grid=(rows // tr,),
        in_specs=[pl.BlockSpec((1, tr, cols), lambda i: (layer, i, 0))],
        out_specs=pl.BlockSpec((tr, cols), lambda i: (i, 0)),
        compiler_params=_cparams("parallel"),
        name="cast_bf16",
    )(w.reshape(w.shape[0], rows, cols))
    return out.reshape(shape)


class _SideCast:
    def __init__(self, w, layer, scale=1.0):
        self.shape = w.shape[1:]
        self.cols = w.shape[-1]
        self.rows = w[0].size // self.cols
        self.w = w.reshape(w.shape[0], self.rows, self.cols)
        self.layer = layer
        self.scale = scale

    def specs(self, n_steps, step):
        slab, rem = divmod(self.rows, n_steps)
        assert rem == 0 and slab % BF16_ROWS == 0, (self.rows, n_steps)
        layer = self.layer
        return (pl.BlockSpec((1, slab, self.cols), lambda *ids: (layer, step(*ids), 0)),
                pl.BlockSpec((slab, self.cols), lambda *ids: (step(*ids), 0)),
                jax.ShapeDtypeStruct((self.rows, self.cols), BF16))


def _with_side_casts(body, n_in, n_out, casts):
    n_c = len(casts)

    def kernel(*refs):
        ins, c_in = refs[:n_in], refs[n_in:n_in + n_c]
        outs = refs[n_in + n_c:n_in + n_c + n_out]
        c_out = refs[n_in + n_c + n_out:n_in + 2 * n_c + n_out]
        for w_ref, o_ref, cast in zip(c_in, c_out, casts):
            w = w_ref[0]
            o_ref[...] = (w if cast.scale == 1.0 else w * cast.scale).astype(o_ref.dtype)
        body(*ins, *outs, *refs[n_in + 2 * n_c + n_out:])

    return kernel


def _side_cast_call(body, casts, n_steps, step, *, out_shape, in_specs, out_specs, args, **kw):
    spec3 = [c.specs(n_steps, step) for c in casts]
    outs = pl.pallas_call(
        _with_side_casts(body, len(in_specs), len(out_shape), casts),
        out_shape=tuple(out_shape) + tuple(s[2] for s in spec3),
        in_specs=list(in_specs) + [s[0] for s in spec3],
        out_specs=tuple(out_specs) + tuple(s[1] for s in spec3),
        **kw,
    )(*args, *[c.w for c in casts])
    n = len(out_shape)
    return outs[:n], [o.reshape(c.shape) for o, c in zip(outs[n:], casts)]


def _ada_kernel(ct_ref, w_ref, b_ref, o_ref, *, n_cond):
    ct = ct_ref[...]
    s = _silu(ct)
    w = w_ref[0]
    o_ref[0] = jnp.zeros(o_ref.shape[1:], F32) + b_ref[0]
    for m in range(n_cond):
        o_ref[0, m:m + 1, :] = jnp.sum(w * s[:, m:m + 1], axis=0, keepdims=True) + b_ref[0]


def _ada(cond_t, w_ada, b_ada, n_cond):
    depth, d, n = w_ada.shape
    tn = 1024
    return pl.pallas_call(
        functools.partial(_ada_kernel, n_cond=n_cond),
        out_shape=jax.ShapeDtypeStruct((depth, COND_ROWS, n), F32),
        grid=(depth, n // tn),
        in_specs=[
            pl.BlockSpec((d, COND_ROWS), lambda l, j: (0, 0)),
            pl.BlockSpec((1, d, tn), lambda l, j: (l, 0, j)),
            pl.BlockSpec((1, 1, tn), lambda l, j: (l, 0, j)),
        ],
        out_specs=pl.BlockSpec((1, COND_ROWS, tn), lambda l, j: (l, 0, j)),
        compiler_params=_cparams("parallel", "parallel"),
        name="ada",
    )(cond_t, w_ada, b_ada.reshape(depth, 1, n))


def _split2(x):
    hi = x.astype(BF16)
    return hi, (x - hi.astype(F32)).astype(BF16)


def _hg_in_kernel(x_ref, g_ref, sh_ref, sc_ref, w_ref, lbl_ref, q_ref, lff_ref, lfb_ref, v_ref, gs_ref,
                  *, ctx, n_lat, layer_j):
    d = x_ref.shape[-1]
    sh = _mod_row(sh_ref, ctx, n_lat)
    sc = _mod_row(sc_ref, ctx, n_lat)
    h = (_rms(x_ref[0], g_ref[...]) * (1.0 + sc) + sh).astype(BF16)
    nc = 2 * MXU_N
    for c0 in range(0, 5 * d, nc):
        part, p0 = divmod(c0, d)
        z = jnp.dot(h, w_ref[:, c0:c0 + nc], preferred_element_type=F32)
        cols = slice(p0, p0 + nc)
        if part == 0:
            q_ref[0, :, cols] = _silu(z).astype(BF16)
        elif part in (1, 2):
            lg = lbl_ref[part - 1, :, cols]
            e = jnp.exp(lg - jnp.max(lg, axis=0, keepdims=True))
            lb = jnp.sum(e[:layer_j + 1], axis=0, keepdims=True) / jnp.sum(e, axis=0, keepdims=True)
            f = lb + (1.0 - lb) * _sigmoid(z)
            (lff_ref if part == 1 else lfb_ref)[0, :, cols] = jnp.log(f)
        elif part == 3:
            v_ref[0, :, cols] = z.astype(BF16)
        else:
            gs_ref[0, :, cols] = _silu(z).astype(BF16)


def _hg_in(x, mods, layer, norm_g, w_in, lb_logits, layer_j, ctx, n_lat, casts=()):
    b, t, d = x.shape
    tm = _row_tile(t, IN_PROJ_ROW_TILE)
    nt = t // tm
    tile = pl.BlockSpec((1, tm, d), lambda bi, i: (bi, i, 0))
    sds = lambda dt: jax.ShapeDtypeStruct((b, t, d), dt)
    return _side_cast_call(
        functools.partial(_hg_in_kernel, ctx=ctx, n_lat=n_lat, layer_j=layer_j),
        casts, b * nt, lambda bi, i: bi * nt + i,
        out_shape=(sds(BF16), sds(F32), sds(F32), sds(BF16), sds(BF16)),
        grid=(b, nt),
        in_specs=[tile, _resident((1, d)), _mod_spec(layer, 0, d), _mod_spec(layer, 1, d),
                  _resident(w_in.shape), _resident(lb_logits.shape)],
        out_specs=(tile, tile, tile, tile, tile),
        compiler_params=_cparams("parallel", "parallel"),
        name="hg_in",
        args=(x, norm_g.reshape(1, d), mods, mods, w_in, lb_logits))


HG_PAIR = 2 * HG_DK
HG_SCAN_BLOCK = 256


def _hg_scan_kernel(qf_ref, lf_ref, vf_ref, qb_ref, lb_ref, vb_ref, s0_ref, of_ref, ob_ref, sfin_ref,
                    st_ref, st16_ref, qe_ref, qs_ref, kebd_ref, kdbd_ref, vbd_ref, dec_ref):
    i = pl.program_id(1)
    tb, d = qf_ref.shape[1], qf_ref.shape[2]
    n_chunks = tb // CHUNK
    n_pairs = d // HG_PAIR

    @pl.when(i == 0)
    def _():
        st_ref[...] = s0_ref[0]
        for dn in range(2):
            for p in range(n_pairs):
                st16_ref[dn, p] = s0_ref[0, dn, p].T.astype(BF16)
        kebd_ref[...] = jnp.zeros(kebd_ref.shape, BF16)
        kdbd_ref[...] = jnp.zeros(kdbd_ref.shape, BF16)
        vbd_ref[...] = jnp.zeros(vbd_ref.shape, BF16)

    row = lax.broadcasted_iota(jnp.int32, (tb, tb), 0)
    col = lax.broadcasted_iota(jnp.int32, (tb, tb), 1)
    same_chunk = (row // CHUNK) == (col // CHUNK)
    r_in = lax.broadcasted_iota(jnp.int32, (CHUNK, 2 * CHUNK), 0)
    c_in = lax.broadcasted_iota(jnp.int32, (CHUNK, 2 * CHUNK), 1) % CHUNK
    dirs = (
        (qf_ref, lf_ref, vf_ref, of_ref, same_chunk & (row >= col), r_in >= c_in, CHUNK // 2, CHUNK - 1),
        (qb_ref, lb_ref, vb_ref, ob_ref, same_chunk & (row <= col), r_in <= c_in, CHUNK // 2 - 1, 0),
    )

    for dn, (q_ref, lfd_ref, v_ref, _, keep_blk, _, ref_row, last_row) in enumerate(dirs):
        tri = jnp.where(keep_blk, 1.0, 0.0).astype(BF16)
        lf = lfd_ref[0]
        b = sum(jnp.dot(tri, part, preferred_element_type=F32) for part in _split2(lf))
        for c in range(n_chunks):
            rows = slice(c * CHUNK, (c + 1) * CHUNK)
            bc = b[rows]
            b_mid = bc[ref_row:ref_row + 1, :]
            b_end = bc[last_row:last_row + 1, :]
            e_q = jnp.exp(bc - b_mid)
            qe = q_ref[0, rows, :].astype(F32) * e_q
            ke = (1.0 - jnp.exp(lf[rows])) * (1.0 / e_q)
            qe_ref[dn, rows, :] = qe.astype(BF16)
            qs_ref[dn, rows, :] = (qe * jnp.exp(b_mid)).astype(BF16)
            ke16 = ke.astype(BF16)
            kd16 = (ke * jnp.exp(b_end - b_mid)).astype(BF16)
            for h in range(2 * n_pairs):
                p, hh = divmod(h, 2)
                cols = slice(h * HG_DK, (h + 1) * HG_DK)
                blk = (slice(hh * CHUNK, (hh + 1) * CHUNK), slice(hh * HG_DK, (hh + 1) * HG_DK))
                kebd_ref[dn, c, p, blk[0], blk[1]] = ke16[:, cols]
                kdbd_ref[dn, c, p, blk[0], blk[1]] = kd16[:, cols]
                vbd_ref[dn, c, p, blk[0], blk[1]] = v_ref[0, rows, cols]
            dec_ref[dn, c:c + 1, :] = jnp.exp(b_end)

    for c in range(n_chunks):
        for dn, (_, _, v_ref, o_ref, _, keep, _, _) in enumerate(dirs):
            cc = c if dn == 0 else n_chunks - 1 - c
            rows = slice(cc * CHUNK, (cc + 1) * CHUNK)
            for p in range(n_pairs):
                pc = slice(p * HG_PAIR, (p + 1) * HG_PAIR)
                scores = lax.dot_general(qe_ref[dn, rows, pc], kebd_ref[dn, cc, p], (((1,), (1,)), ((), ())),
                                         preferred_element_type=F32)
                scores = jnp.where(keep, scores, 0.0).astype(BF16)
                o = jnp.dot(scores, vbd_ref[dn, cc, p], preferred_element_type=F32)
                o = o + jnp.dot(qs_ref[dn, rows, pc], st16_ref[dn, p], preferred_element_type=F32)
                o_ref[0, rows, pc] = o.astype(BF16)
                v_stack = jnp.concatenate(
                    [v_ref[0, rows, p * HG_PAIR:p * HG_PAIR + HG_DK],
                     v_ref[0, rows, p * HG_PAIR + HG_DK:(p + 1) * HG_PAIR]], axis=0)
                kv = lax.dot_general(v_stack, kdbd_ref[dn, cc, p], (((0,), (0,)), ((), ())),
                                     preferred_element_type=F32)
                for hh in range(2):
                    sl = slice(hh * HG_DK, (hh + 1) * HG_DK)
                    cols = slice(p * HG_PAIR + hh * HG_DK, p * HG_PAIR + (hh + 1) * HG_DK)
                    new = st_ref[dn, p, sl, sl] * dec_ref[dn, cc:cc + 1, cols] + kv[:, sl]
                    st_ref[dn, p, sl, sl] = new
                    st16_ref[dn, p, sl, sl] = new.T.astype(BF16)

    @pl.when(i == pl.num_programs(1) - 1)
    def _():
        sfin_ref[0] = st_ref[...]


def _hg_scan(q, lf_f, lf_b, v, s0, casts=()):
    b, t, d = q.shape
    tb = HG_SCAN_BLOCK
    nblk = t // tb
    fwd = pl.BlockSpec((1, tb, d), lambda bi, i: (bi, i, 0))
    bwd = pl.BlockSpec((1, tb, d), lambda bi, i: (bi, nblk - 1 - i, 0))
    st_spec = pl.BlockSpec((1,) + s0.shape[1:], lambda bi, i: (bi, 0, 0, 0, 0))
    blockdiag = pltpu.VMEM((2, tb // CHUNK, d // HG_PAIR, 2 * CHUNK, HG_PAIR), BF16)
    return _side_cast_call(
        _hg_scan_kernel, casts, b * nblk, lambda bi, i: bi * nblk + i,
        out_shape=(jax.ShapeDtypeStruct((b, t, d), BF16), jax.ShapeDtypeStruct((b, t, d), BF16),
                   jax.ShapeDtypeStruct(s0.shape, F32)),
        grid=(b, nblk),
        in_specs=[fwd, fwd, fwd, bwd, bwd, bwd, st_spec],
        out_specs=(fwd, bwd, st_spec),
        scratch_shapes=[pltpu.VMEM(s0.shape[1:], F32), pltpu.VMEM(s0.shape[1:], BF16),
                        pltpu.VMEM((2, tb, d), BF16), pltpu.VMEM((2, tb, d), BF16),
                        blockdiag, blockdiag, blockdiag, pltpu.VMEM((2, tb // CHUNK, d), F32)],
        compiler_params=_cparams("parallel", "arbitrary"),
        name="hg_scan",
        args=(q, lf_f, v, q, lf_b, v, s0))


def _lru_in_kernel(x_ref, g_ref, sh_ref, sc_ref, w_ref, gg_ref, xr_ref, *, ctx, n_lat):
    d = x_ref.shape[-1]
    sh = _mod_row(sh_ref, ctx, n_lat)
    sc = _mod_row(sc_ref, ctx, n_lat)
    h = (_rms(x_ref[0], g_ref[...]) * (1.0 + sc) + sh).astype(BF16)
    nc = 2 * MXU_N
    for c0 in range(0, 2 * d, nc):
        part, p0 = divmod(c0, d)
        z = jnp.dot(h, w_ref[:, c0:c0 + nc], preferred_element_type=F32)
        if part == 0:
            gg_ref[0, :, p0:p0 + nc] = _gelu_tanh(z).astype(BF16)
        else:
            xr_ref[0, :, p0:p0 + nc] = z


def _lru_in(x, mods, layer, norm_g, w_in, ctx, n_lat, casts=()):
    b, t, d = x.shape
    tm = _row_tile(t, IN_PROJ_ROW_TILE)
    nt = t // tm
    tile = pl.BlockSpec((1, tm, d), lambda bi, i: (bi, i, 0))
    return _side_cast_call(
        functools.partial(_lru_in_kernel, ctx=ctx, n_lat=n_lat),
        casts, b * nt, lambda bi, i: bi * nt + i,
        out_shape=(jax.ShapeDtypeStruct((b, t, d), BF16), jax.ShapeDtypeStruct((b, t, d), F32)),
        grid=(b, nt),
        in_specs=[tile, _resident((1, d)), _mod_spec(layer, 0, d), _mod_spec(layer, 1, d),
                  _resident(w_in.shape)],
        out_specs=(tile, tile),
        compiler_params=_cparams("parallel", "parallel"),
        name="lru_in",
        args=(x, norm_g.reshape(1, d), mods, mods, w_in))


LRU_COL_BLOCK = BF16_ROWS
LRU_GATE_ROWS = MXU_N
LRU_SCAN_SEGS = 2
LOG2_E = 1.4426950408889634
LN_2 = 0.6931471805599453
HALO = SUBLANES


def _softplus(x):
    y = jnp.exp(-jnp.abs(x))
    u = 1.0 + y
    log1p = jnp.where(u == 1.0, y, jnp.log(u) * (y / (u - 1.0)))
    return jnp.maximum(x, 0.0) + log1p


def _lru_scan_kernel(xf_ref, xfp_ref, xfn_ref, xb_ref, xbp_ref, xbn_ref, cw_ref, cb_ref, wa_ref, wx_ref,
                     ba_ref, bx_ref, lam_ref, h0_ref, hf_ref, hb_ref, hfin_ref, xpad_ref, a_ref, u_ref,
                     hl_ref, pr_ref, carry_ref):
    j = pl.program_id(2)
    nj = pl.num_programs(2)
    _, nr, wb, cb = xf_ref.shape

    @pl.when(j == 0)
    def _():
        carry_ref[...] = h0_ref[:, 0]

    col = lax.broadcasted_iota(jnp.int32, (wb, cb), 0)
    left = LRU_CONV_W - 1 - (LRU_CONV_W - 1) // 2
    right = LRU_CONV_W - 1 - left
    dirs = ((xf_ref, xfp_ref, xfn_ref, j), (xb_ref, xbp_ref, xbn_ref, nj - 1 - j))
    for dn, (x_ref, xp_ref, xn_ref, jb) in enumerate(dirs):
        for k in range(left):
            edge = jnp.where(jb > 0, xp_ref[0, HALO - left + k, wb - 1:wb, :], 0.0)
            xpad_ref[dn, k] = jnp.where(col == 0, edge, pltpu.roll(x_ref[0, nr - left + k], 1, axis=0))
        xpad_ref[dn, left:left + nr] = x_ref[0]
        for k in range(right):
            edge = jnp.where(jb < nj - 1, xn_ref[0, k, 0:1, :], 0.0)
            xpad_ref[dn, left + nr + k] = jnp.where(col == wb - 1, edge,
                                                    pltpu.roll(x_ref[0, k], wb - 1, axis=0))

    cw = [cw_ref[k:k + 1, :] for k in range(LRU_CONV_W)]
    cbias = cb_ref[...]
    k2 = [(-0.5 * LRU_C * LOG2_E) * _softplus(-lam_ref[dn]) for dn in range(2)]
    half_ba = [0.5 * ba_ref[dn] for dn in range(2)]
    half_bx = [0.5 * bx_ref[dn] for dn in range(2)]
    rc = LRU_GATE_ROWS // wb

    def gate_chunk(ci, carry):
        r0 = pl.multiple_of(ci * rc, rc)
        for dn in range(2):
            xc = cbias + sum(cw[k] * xpad_ref[dn, pl.ds(r0 + k, rc)] for k in range(LRU_CONV_W))
            xc = xc.reshape(rc * wb, cb)
            xcb = xc.astype(BF16)
            tr = jnp.tanh(jnp.dot(xcb, wa_ref[dn, 0], preferred_element_type=F32) + half_ba[dn])
            ti = jnp.tanh(jnp.dot(xcb, wx_ref[dn, 0], preferred_element_type=F32) + half_bx[dn])
            log2_a = k2[dn] * tr + k2[dn]
            a = jnp.exp2(log2_a)
            q2 = (a * a * 0.25 + 0.25) * jnp.tanh(log2_a * (-LN_2))
            half_mult = jnp.where(q2 > 0.0, q2 * lax.rsqrt(q2), 0.0)
            a_ref[dn, pl.ds(r0, rc)] = a.reshape(rc, wb, cb)
            u_ref[dn, pl.ds(r0, rc)] = (half_mult * (xc * ti + xc)).reshape(rc, wb, cb)
        return carry

    lax.fori_loop(0, nr // rc, gate_chunk, 0)

    n_seg = LRU_SCAN_SEGS
    seg = nr // n_seg

    def scan_step(rr, carry):
        out = []
        for dn in range(2):
            for s in range(n_seg):
                h, p = carry[dn * n_seg + s]
                r = s * seg + (rr if dn == 0 else seg - 1 - rr)
                a = a_ref[dn, r]
                h = a * h + u_ref[dn, r]
                p = a * p
                hl_ref[dn, r] = h
                pr_ref[dn, r] = p
                out.append((h, p))
        return tuple(out)

    zero = jnp.zeros((wb, cb), F32)
    one = jnp.ones((wb, cb), F32)
    ends = lax.fori_loop(0, seg, scan_step, ((zero, one),) * (2 * n_seg), unroll=2)

    for dn, o_ref in enumerate((hf_ref, hb_ref)):
        c_in = carry_ref[dn]
        c_units = [zero] * n_seg
        wls = range(wb) if dn == 0 else range(wb - 1, -1, -1)
        segs = range(n_seg) if dn == 0 else range(n_seg - 1, -1, -1)
        for wl in wls:
            for s in segs:
                h_end, p_end = ends[dn * n_seg + s]
                c_units[s] = jnp.where(col == wl, c_in, c_units[s])
                c_in = p_end[wl:wl + 1, :] * c_in + h_end[wl:wl + 1, :]
        carry_ref[dn] = c_in
        for s in range(n_seg):
            rows = slice(s * seg, (s + 1) * seg)
            o_ref[0, rows] = (hl_ref[dn, rows] + pr_ref[dn, rows] * c_units[s]).astype(o_ref.dtype)

    @pl.when(j == nj - 1)
    def _():
        hfin_ref[:, 0] = carry_ref[...]


def _lru_scan(xr, conv_w, conv_b, wa, wx, ba, bx, lam, h0, grid_w):
    b, t, c = xr.shape
    nr = t // grid_w
    cb = c // LRU_BLOCKS
    wb = LRU_COL_BLOCK
    nj = grid_w // wb
    x4 = xr.reshape(b, nr, grid_w, c)
    up = lambda j: j
    down = lambda j: nj - 1 - j
    main = lambda cj: pl.BlockSpec((1, nr, wb, cb), lambda bi, n, j: (bi, 0, cj(j), n))
    prev = lambda cj: pl.BlockSpec((1, HALO, wb, cb),
                                   lambda bi, n, j: (bi, nr // HALO - 1, jnp.maximum(cj(j) - 1, 0), n))
    nxt = lambda cj: pl.BlockSpec((1, HALO, wb, cb),
                                  lambda bi, n, j: (bi, 0, jnp.minimum(cj(j) + 1, nj - 1), n))
    chan = lambda rows: pl.BlockSpec((rows, cb), lambda bi, n, j: (0, n))
    per_dir = pl.BlockSpec((2, 1, cb), lambda bi, n, j: (0, 0, n))
    gate_w = pl.BlockSpec((2, 1, cb, cb), lambda bi, n, j: (0, n, 0, 0))
    state = pl.BlockSpec((2, 1, 1, cb), lambda bi, n, j: (0, bi, 0, n))
    hf, hb, hfin = pl.pallas_call(
        _lru_scan_kernel,
        out_shape=(jax.ShapeDtypeStruct(x4.shape, BF16), jax.ShapeDtypeStruct(x4.shape, BF16),
                   jax.ShapeDtypeStruct(h0.shape, F32)),
        grid=(b, LRU_BLOCKS, nj),
        in_specs=[main(up), prev(up), nxt(up), main(down), prev(down), nxt(down),
                  chan(LRU_CONV_W), chan(1), gate_w, gate_w, per_dir, per_dir, per_dir, state],
        out_specs=(main(up), main(down), state),
        scratch_shapes=[pltpu.VMEM((2, nr + LRU_CONV_W - 1, wb, cb), F32)]
        + [pltpu.VMEM((2, nr, wb, cb), F32)] * 4 + [pltpu.VMEM((2, 1, cb), F32)],
        compiler_params=_cparams("parallel", "parallel", "arbitrary"),
        name="lru_scan",
    )(x4, x4, x4, x4, x4, x4, conv_w, conv_b.reshape(1, c), wa, wx,
      ba.reshape(2, 1, c), bx.reshape(2, 1, c), lam.reshape(2, 1, c), h0)
    return hf.reshape(b, t, c), hb.reshape(b, t, c), hfin


FFN_ROW_TILE = 512


def _mix_ffn_kernel(af_ref, ab_ref, gs_ref, xm_ref, afp_ref, abp_ref, gsp_ref, xp_ref, afn_ref, abn_ref,
                    gsn_ref, xn_ref, gain_ref, wo_ref, gate1_ref, g_ref, sh_ref, sc_ref, gate_ref, wup_ref,
                    cw_ref, cb_ref, wdn_ref, fg_ref, out_ref, y_ref, lat_ref, h_ref, act_ref,
                    *, head_norm, ctx, n_lat, final_norm):
    i = pl.program_id(1)
    tm, d = xm_ref.shape[1], xm_ref.shape[2]
    dff = wdn_ref.shape[0]
    lo = slice(BF16_ROWS - HALO, BF16_ROWS)
    hi = slice(0, HALO)

    def readout_in(af, ab, gs):
        o = af.astype(F32) + ab.astype(F32)
        if head_norm:
            dv = gain_ref.shape[-1]
            o = jnp.concatenate([_rms(o[:, c0:c0 + dv], gain_ref[...]) for c0 in range(0, d, dv)], axis=-1)
        return o * gs.astype(F32)

    g = g_ref[...]
    sh = _mod_row(sh_ref, ctx, n_lat)
    sc = _mod_row(sc_ref, ctx, n_lat)
    mod = lambda x: _rms(x, g) * (1.0 + sc) + sh
    rows = tm + 2 * HALO
    y_ref[...] = jnp.concatenate(
        [readout_in(afp_ref[0, lo], abp_ref[0, lo], gsp_ref[0, lo]),
         readout_in(af_ref[0], ab_ref[0], gs_ref[0]),
         readout_in(afn_ref[0, hi], abn_ref[0, hi], gsn_ref[0, hi])], axis=0).astype(BF16)
    x_all = jnp.concatenate([xp_ref[0, lo], xm_ref[0], xn_ref[0, hi]], axis=0)
    lat_ref[...] = x_all + _mod_row(gate1_ref, ctx, n_lat) * jnp.dot(y_ref[...], wo_ref[...],
                                                                     preferred_element_type=F32)
    hp = jnp.where(i > 0, mod(lat_ref[0:HALO]), 0.0)
    hn = jnp.where(i < pl.num_programs(1) - 1, mod(lat_ref[HALO + tm:]), 0.0)
    h_ref[...] = jnp.concatenate([hp, mod(lat_ref[HALO:HALO + tm]), hn], axis=0).astype(BF16)
    nc = MXU_N
    for c0 in range(0, dff, nc):
        halves = []
        for base in (c0, dff + c0):
            u = jnp.dot(h_ref[...], wup_ref[:, base:base + nc], preferred_element_type=F32)
            cols = slice(base, base + nc)
            conv = (cb_ref[:, cols]
                    + cw_ref[0:1, cols] * pltpu.roll(u, 1, axis=0)[HALO:HALO + tm]
                    + cw_ref[1:2, cols] * u[HALO:HALO + tm]
                    + cw_ref[2:3, cols] * pltpu.roll(u, rows - 1, axis=0)[HALO:HALO + tm])
            halves.append(conv)
        act_ref[:, c0:c0 + nc] = (_silu(halves[0]) * halves[1]).astype(BF16)
    half = tm // 2
    for r0 in (0, half):
        z = jnp.dot(act_ref[r0:r0 + half], wdn_ref[...], preferred_element_type=F32)
        y = lat_ref[HALO + r0:HALO + r0 + half] + _mod_row(gate_ref, ctx, n_lat) * z
        out_ref[0, r0:r0 + half] = _rms(y, fg_ref[...]) if final_norm else y


def _mix_ffn(a_f, a_b, gs, x, mods, layer, head_gain, w_out, norm_g, w_up, conv_w, conv_b, w_down, final_g,
             ctx, n_lat, final_norm, casts=()):
    b, t, d = x.shape
    dff = w_down.shape[0]
    tm = _row_tile(t, FFN_ROW_TILE)
    nt = t // tm
    per = tm // BF16_ROWS
    tile = pl.BlockSpec((1, tm, d), lambda bi, i: (bi, i, 0))
    prev = pl.BlockSpec((1, BF16_ROWS, d), lambda bi, i: (bi, jnp.maximum(i * per - 1, 0), 0))
    nxt = pl.BlockSpec((1, BF16_ROWS, d), lambda bi, i: (bi, jnp.minimum((i + 1) * per, t // BF16_ROWS - 1), 0))
    head_norm = head_gain is not None
    gain = head_gain.reshape(1, -1) if head_norm else jnp.ones((1, LANES), F32)
    rows = tm + 2 * HALO
    (out,), cast_w = _side_cast_call(
        functools.partial(_mix_ffn_kernel, head_norm=head_norm, ctx=ctx, n_lat=n_lat, final_norm=final_norm),
        casts, b * nt, lambda bi, i: bi * nt + i,
        out_shape=(jax.ShapeDtypeStruct((b, t, d), F32),),
        grid=(b, nt),
        in_specs=[tile, tile, tile, tile, prev, prev, prev, prev, nxt, nxt, nxt, nxt,
                  _resident(gain.shape), _resident(w_out.shape), _mod_spec(layer, 2, d),
                  _resident((1, d)), _mod_spec(layer, 3, d), _mod_spec(layer, 4, d), _mod_spec(layer, 5, d),
                  _resident(w_up.shape), _resident(conv_w.shape), _resident((1, 2 * dff)),
                  _resident(w_down.shape), _resident((1, d))],
        out_specs=(tile,),
        scratch_shapes=[pltpu.VMEM((rows, d), BF16), pltpu.VMEM((rows, d), F32), pltpu.VMEM((rows, d), BF16),
                        pltpu.VMEM((tm, dff), BF16)],
        compiler_params=_cparams("parallel", "parallel"),
        name="mix_ffn",
        args=(a_f, a_b, gs, x, a_f, a_b, gs, x, a_f, a_b, gs, x, gain, w_out, mods,
              norm_g.reshape(1, d), mods, mods, mods, w_up, conv_w, conv_b.reshape(1, 2 * dff), w_down,
              final_g.reshape(1, d)))
    return out, cast_w


def kernel(x, c, ctx, c_ctx, w_ada, b_ada, norm_g, hg_w_in, hg_lb_logits, hg_gnorm, hg_w_out, lru_w_in,
           lru_conv_w, lru_conv_b, lru_wa, lru_ba, lru_wx, lru_bx, lru_lambda, lru_w_out, ffn_w_up,
           ffn_conv_w, ffn_conv_b, ffn_w_down, final_g):
    nb, _, d = x.shape
    depth = w_ada.shape[0]
    n_mixers = 2

    weights = {"hg_in": (hg_w_in, 1.0), "hg_out": (hg_w_out, 1.0), "lru_in": (lru_w_in, 1.0),
               "lru_out": (lru_w_out, 1.0), "lru_wa": (lru_wa, 0.5), "lru_wx": (lru_wx, 0.5),
               "ffn_up": (ffn_w_up, 1.0), "ffn_down": (ffn_w_down, 1.0)}
    ready = {}

    def bf(name, idx):
        if (name, idx) not in ready:
            ready[name, idx] = _layer_bf16(weights[name][0], idx, weights[name][1])
        return ready[name, idx]

    def hosted(call, *keys):
        keys = [k for k in keys if k not in ready]
        outs, cast_w = call(casts=[_SideCast(weights[n][0], i, weights[n][1]) for n, i in keys])
        ready.update(zip(keys, cast_w))
        return outs

    def mixer_keys(l):
        j = l // n_mixers
        names = ("hg_in", "hg_out") if l % n_mixers == 0 else ("lru_in", "lru_out", "lru_wa", "lru_wx")
        return [(n, j) for n in names]

    cond = jnp.concatenate([c, c_ctx[None, :], jnp.zeros((COND_ROWS - nb - 1, d), F32)], axis=0)
    mods = _ada(cond.T, w_ada, b_ada, nb + 1)

    lat, cx = x, ctx
    for l in range(depth):
        last = l == depth - 1
        j = l // n_mixers
        if l % n_mixers == 0:
            w_in, w_out = bf("hg_in", j), bf("hg_out", j)
            s0 = jnp.zeros((nb, 2, d // HG_PAIR, HG_PAIR, HG_PAIR), F32)
            lb_logits = jnp.swapaxes(hg_lb_logits, 0, 1)
            hg_in = functools.partial(_hg_in, mods=mods, layer=l, norm_g=norm_g[l, 0], w_in=w_in,
                                      lb_logits=lb_logits, layer_j=j, n_lat=nb)
            pc, _ = hg_in(cx, ctx=True)
            (oc_f, oc_b, s_ctx), _ = _hg_scan(pc[0], pc[1], pc[2], pc[3], s0)
            pl_ = hosted(functools.partial(hg_in, lat, ctx=False), ("ffn_down", l))
            ol_f, ol_b, _ = hosted(functools.partial(_hg_scan, pl_[0], pl_[1], pl_[2], pl_[3], s_ctx),
                                   ("ffn_up", l))
            mix_lat = (ol_f, ol_b, pl_[4])
            mix_ctx = (oc_f, oc_b, pc[4])
            head_gain = hg_gnorm[j]
        else:
            w_in, w_out = bf("lru_in", j), bf("lru_out", j)
            scan = functools.partial(_lru_scan, conv_w=lru_conv_w[j], conv_b=lru_conv_b[j],
                                     wa=bf("lru_wa", j), wx=bf("lru_wx", j),
                                     ba=lru_ba[j], bx=lru_bx[j], lam=lru_lambda[j])
            lru_in = functools.partial(_lru_in, mods=mods, layer=l, norm_g=norm_g[l, 0], w_in=w_in, n_lat=nb)
            wc = LRU_COL_BLOCK
            t_ctx = cx.shape[1]
            flip = lambda a: jnp.swapaxes(a.reshape(nb, wc, t_ctx // wc, d), 1, 2).reshape(nb, t_ctx, d)
            unflip = lambda a: jnp.swapaxes(a.reshape(nb, t_ctx // wc, wc, d), 1, 2).reshape(nb, t_ctx, d)
            (gg_c, xr_c), _ = lru_in(cx, ctx=True)
            hc_f, hc_b, h_ctx = scan(flip(xr_c), h0=jnp.zeros((2, nb, 1, d), F32), grid_w=wc)
            gg_l, xr_l = hosted(functools.partial(lru_in, lat, ctx=False), ("ffn_down", l))
            hl_f, hl_b, _ = scan(xr_l, h0=h_ctx, grid_w=GRID_W)
            mix_lat = (hl_f, hl_b, gg_l)
            mix_ctx = (hc_f, hc_b, gg_c)
            head_gain = None
        mix_ffn = functools.partial(_mix_ffn, mods=mods, layer=l, head_gain=head_gain, w_out=w_out,
                                    norm_g=norm_g[l, 1], w_up=bf("ffn_up", l), conv_w=ffn_conv_w[l],
                                    conv_b=ffn_conv_b[l], w_down=bf("ffn_down", l), final_g=final_g, n_lat=nb)
        nxt = [] if last else mixer_keys(l + 1) + ([("ffn_up", l + 1)] if (l + 1) % n_mixers else [])
        lat = hosted(functools.partial(mix_ffn, *mix_lat, lat, ctx=False, final_norm=last), *nxt)
        if not last:
            if head_gain is None:
                mix_ctx = (unflip(mix_ctx[0]), unflip(mix_ctx[1]), mix_ctx[2])
            cx, _ = mix_ffn(*mix_ctx, cx, ctx=True, final_norm=False)
    return lat
```

```python
import functools

import jax
import jax.numpy as jnp
from jax import lax
from jax.experimental import pallas as pl
from jax.experimental.pallas import tpu as pltpu

F32 = jnp.float32
BF16 = jnp.bfloat16

EPS = 1e-6
GRID_W = 64
HG_DK = 128
CHUNK = 64
LRU_BLOCKS = 4
LRU_C = 8.0
LRU_CONV_W = 4
FFN_CONV_W = 3
N_MOD = 6

SUBLANES = 8
LANES = 128
BF16_ROWS = 16
MXU_N = 256
VMEM_LIMIT_BYTES = 56 * 1024 * 1024
COND_ROWS = SUBLANES


def _cparams(*sem):
    return pltpu.CompilerParams(dimension_semantics=sem, vmem_limit_bytes=VMEM_LIMIT_BYTES)


def _resident(shape):
    zeros = (0,) * len(shape)
    return pl.BlockSpec(shape, lambda *_: zeros, pipeline_mode=pl.Buffered(1))


IN_PROJ_ROW_TILE = 1024


def _row_tile(t, largest):
    tm = largest
    while t % tm:
        tm //= 2
    return tm


def _sigmoid(x):
    return 0.5 + 0.5 * jnp.tanh(0.5 * x)


def _silu(x):
    return x * _sigmoid(x)


def _gelu_tanh(x):
    return 0.5 * x * (1.0 + jnp.tanh(0.7978845608028654 * (x + 0.044715 * (x * x * x))))


def _rms(x, g):
    return x * lax.rsqrt(jnp.mean(x * x, axis=-1, keepdims=True) + EPS) * g


def _mod_row(mod_ref, ctx, n_lat):
    if ctx:
        return mod_ref[0, n_lat:n_lat + 1, :]
    return mod_ref[0, pl.ds(pl.program_id(0), 1), :]


def _mod_spec(layer, k, d):
    return pl.BlockSpec((1, COND_ROWS, d), lambda *_: (layer, 0, k))


CAST_BLOCK_BYTES = 4 * 1024 * 1024


def _cast_kernel(w_ref, o_ref, *, scale):
    w = w_ref[0]
    o_ref[...] = (w if scale == 1.0 else w * scale).astype(o_ref.dtype)


def _layer_bf16(w, layer, scale=1.0):
    shape = w.shape[1:]
    cols = shape[-1]
    rows = w[0].size // cols
    tr = rows
    while tr * cols * 4 > CAST_BLOCK_BYTES and tr % (2 * BF16_ROWS) == 0:
        tr //= 2
    out = pl.pallas_call(
        functools.partial(_cast_kernel, scale=scale),
        out_shape=jax.ShapeDtypeStruct((rows, cols), BF16),
        grid=(rows // tr,),
        in_specs=[pl.BlockSpec((1, tr, cols), lambda i: (layer, i, 0))],
        out_specs=pl.BlockSpec((tr, cols), lambda i: (i, 0)),
        compiler_params=_cparams("parallel"),
        name="cast_bf16",
    )(w.reshape(w.shape[0], rows, cols))
    return out.reshape(shape)


class _SideCast:
    def __init__(self, w, layer, scale=1.0):
        self.shape = w.shape[1:]
        self.cols = w.shape[-1]
        self.rows = w[0].size // self.cols
        self.w = w.reshape(w.shape[0], self.rows, self.cols)
        self.layer = layer
        self.scale = scale

    def specs(self, n_steps, step):
        slab, rem = divmod(self.rows, n_steps)
        assert rem == 0 and slab % BF16_ROWS == 0, (self.rows, n_steps)
        layer = self.layer
        return (pl.BlockSpec((1, slab, self.cols), lambda *ids: (layer, step(*ids), 0)),
                pl.BlockSpec((slab, self.cols), lambda *ids: (step(*ids), 0)),
                jax.ShapeDtypeStruct((self.rows, self.cols), BF16))


def _with_side_casts(body, n_in, n_out, casts):
    n_c = len(casts)

    def kernel(*refs):
        ins, c_in = refs[:n_in], refs[n_in:n_in + n_c]
        outs = refs[n_in + n_c:n_in + n_c + n_out]
        c_out = refs[n_in + n_c + n_out:n_in + 2 * n_c + n_out]
        for w_ref, o_ref, cast in zip(c_in, c_out, casts):
            w = w_ref[0]
            o_ref[...] = (w if cast.scale == 1.0 else w * cast.scale).astype(o_ref.dtype)
        body(*ins, *outs, *refs[n_in + 2 * n_c + n_out:])

    return kernel


def _side_cast_call(body, casts, n_steps, step, *, out_shape, in_specs, out_specs, args, **kw):
    spec3 = [c.specs(n_steps, step) for c in casts]
    outs = pl.pallas_call(
        _with_side_casts(body, len(in_specs), len(out_shape), casts),
        out_shape=tuple(out_shape) + tuple(s[2] for s in spec3),
        in_specs=list(in_specs) + [s[0] for s in spec3],
        out_specs=tuple(out_specs) + tuple(s[1] for s in spec3),
        **kw,
    )(*args, *[c.w for c in casts])
    n = len(out_shape)
    return outs[:n], [o.reshape(c.shape) for o, c in zip(outs[n:], casts)]


def _ada_kernel(ct_ref, w_ref, b_ref, o_ref, *, n_cond):
    ct = ct_ref[...]
    s = _silu(ct)
    w = w_ref[0]
    o_ref[0] = jnp.zeros(o_ref.shape[1:], F32) + b_ref[0]
    for m in range(n_cond):
        o_ref[0, m:m + 1, :] = jnp.sum(w * s[:, m:m + 1], axis=0, keepdims=True) + b_ref[0]


def _ada(cond_t, w_ada, b_ada, n_cond):
    depth, d, n = w_ada.shape
    tn = 1024
    return pl.pallas_call(
        functools.partial(_ada_kernel, n_cond=n_cond),
        out_shape=jax.ShapeDtypeStruct((depth, COND_ROWS, n), F32),
        grid=(depth, n // tn),
        in_specs=[
            pl.BlockSpec((d, COND_ROWS), lambda l, j: (0, 0)),
            pl.BlockSpec((1, d, tn), lambda l, j: (l, 0, j)),
            pl.BlockSpec((1, 1, tn), lambda l, j: (l, 0, j)),
        ],
        out_specs=pl.BlockSpec((1, COND_ROWS, tn), lambda l, j: (l, 0, j)),
        compiler_params=_cparams("parallel", "parallel"),
        name="ada",
    )(cond_t, w_ada, b_ada.reshape(depth, 1, n))


def _split2(x):
    hi = x.astype(BF16)
    return hi, (x - hi.astype(F32)).astype(BF16)


def _hg_in_kernel(x_ref, g_ref, sh_ref, sc_ref, w_ref, lbl_ref, q_ref, lff_ref, lfb_ref, v_ref, gs_ref,
                  *, ctx, n_lat, layer_j):
    d = x_ref.shape[-1]
    sh = _mod_row(sh_ref, ctx, n_lat)
    sc = _mod_row(sc_ref, ctx, n_lat)
    h = (_rms(x_ref[0], g_ref[...] * (1.0 + sc)) + sh).astype(BF16)
    nc = 2 * MXU_N
    for c0 in range(0, 5 * d, nc):
        part, p0 = divmod(c0, d)
        z = jnp.dot(h, w_ref[:, c0:c0 + nc], preferred_element_type=F32)
        cols = slice(p0, p0 + nc)
        if part == 0:
            q_ref[0, :, cols] = _silu(z).astype(BF16)
        elif part in (1, 2):
            lg = lbl_ref[part - 1, :, cols]
            e = jnp.exp(lg - jnp.max(lg, axis=0, keepdims=True))
            lb = jnp.sum(e[:layer_j + 1], axis=0, keepdims=True) / jnp.sum(e, axis=0, keepdims=True)
            f = lb + (1.0 - lb) * _sigmoid(z)
            (lff_ref if part == 1 else lfb_ref)[0, :, cols] = jnp.log(f)
        elif part == 3:
            v_ref[0, :, cols] = z.astype(BF16)
        else:
            gs_ref[0, :, cols] = _silu(z).astype(BF16)


def _hg_in(x, mods, layer, norm_g, w_in, lb_logits, layer_j, ctx, n_lat, casts=()):
    b, t, d = x.shape
    tm = _row_tile(t, IN_PROJ_ROW_TILE)
    nt = t // tm
    tile = pl.BlockSpec((1, tm, d), lambda bi, i: (bi, i, 0))
    sds = lambda dt: jax.ShapeDtypeStruct((b, t, d), dt)
    return _side_cast_call(
        functools.partial(_hg_in_kernel, ctx=ctx, n_lat=n_lat, layer_j=layer_j),
        casts, b * nt, lambda bi, i: bi * nt + i,
        out_shape=(sds(BF16), sds(F32), sds(F32), sds(BF16), sds(BF16)),
        grid=(b, nt),
        in_specs=[tile, _resident((1, d)), _mod_spec(layer, 0, d), _mod_spec(layer, 1, d),
                  _resident(w_in.shape), _resident(lb_logits.shape)],
        out_specs=(tile, tile, tile, tile, tile),
        compiler_params=_cparams("parallel", "parallel"),
        name="hg_in",
        args=(x, norm_g.reshape(1, d), mods, mods, w_in, lb_logits))


HG_PAIR = 2 * HG_DK
HG_SCAN_BLOCK = 512


def _hg_scan_kernel(qf_ref, lf_ref, vf_ref, qb_ref, lb_ref, vb_ref, s0_ref, of_ref, ob_ref, sfin_ref,
                    st_ref, st16_ref, qe_ref, qs_ref, kebd_ref, kdbd_ref, vbd_ref, dec_ref):
    i = pl.program_id(1)
    tb, d = qf_ref.shape[1], qf_ref.shape[2]
    n_chunks = tb // CHUNK
    n_pairs = d // HG_PAIR

    @pl.when(i == 0)
    def _():
        st_ref[...] = s0_ref[0]
        for dn in range(2):
            for p in range(n_pairs):
                st16_ref[dn, p] = s0_ref[0, dn, p].T.astype(BF16)
        kebd_ref[...] = jnp.zeros(kebd_ref.shape, BF16)
        kdbd_ref[...] = jnp.zeros(kdbd_ref.shape, BF16)
        vbd_ref[...] = jnp.zeros(vbd_ref.shape, BF16)

    grp = min(tb, MXU_N)
    row = lax.broadcasted_iota(jnp.int32, (grp, grp), 0)
    col = lax.broadcasted_iota(jnp.int32, (grp, grp), 1)
    same_chunk = (row // CHUNK) == (col // CHUNK)
    r_in = lax.broadcasted_iota(jnp.int32, (CHUNK, 2 * CHUNK), 0)
    c_in = lax.broadcasted_iota(jnp.int32, (CHUNK, 2 * CHUNK), 1) % CHUNK
    dirs = (
        (qf_ref, lf_ref, vf_ref, of_ref, same_chunk & (row >= col), r_in >= c_in, CHUNK // 2, CHUNK - 1),
        (qb_ref, lb_ref, vb_ref, ob_ref, same_chunk & (row <= col), r_in <= c_in, CHUNK // 2 - 1, 0),
    )

    for dn, (q_ref, lfd_ref, v_ref, _, keep_blk, _, ref_row, last_row) in enumerate(dirs):
        tri = jnp.where(keep_blk, 1.0, 0.0).astype(BF16)
        lf = lfd_ref[0]
        b = jnp.concatenate(
            [sum(jnp.dot(tri, part, preferred_element_type=F32) for part in _split2(lf[r0:r0 + grp]))
             for r0 in range(0, tb, grp)], axis=0)
        for c in range(n_chunks):
            rows = slice(c * CHUNK, (c + 1) * CHUNK)
            bc = b[rows]
            b_mid = bc[ref_row:ref_row + 1, :]
            b_end = bc[last_row:last_row + 1, :]
            e_q = jnp.exp(bc - b_mid)
            qe = q_ref[0, rows, :].astype(F32) * e_q
            ke = (1.0 - jnp.exp(lf[rows])) * (1.0 / e_q)
            qe_ref[dn, rows, :] = qe.astype(BF16)
            qs_ref[dn, rows, :] = (qe * jnp.exp(b_mid)).astype(BF16)
            ke16 = ke.astype(BF16)
            kd16 = (ke * jnp.exp(b_end - b_mid)).astype(BF16)
            for h in range(2 * n_pairs):
                p, hh = divmod(h, 2)
                cols = slice(h * HG_DK, (h + 1) * HG_DK)
                blk = (slice(hh * CHUNK, (hh + 1) * CHUNK), slice(hh * HG_DK, (hh + 1) * HG_DK))
                kebd_ref[dn, c, p, blk[0], blk[1]] = ke16[:, cols]
                kdbd_ref[dn, c, p, blk[0], blk[1]] = kd16[:, cols]
                vbd_ref[dn, c, p, blk[0], blk[1]] = v_ref[0, rows, cols]
            dec_ref[dn, c:c + 1, :] = jnp.exp(b_end)

    for c in range(n_chunks):
        for dn, (_, _, v_ref, o_ref, _, keep, _, _) in enumerate(dirs):
            cc = c if dn == 0 else n_chunks - 1 - c
            rows = slice(cc * CHUNK, (cc + 1) * CHUNK)
            for p in range(n_pairs):
                pc = slice(p * HG_PAIR, (p + 1) * HG_PAIR)
                scores = lax.dot_general(qe_ref[dn, rows, pc], kebd_ref[dn, cc, p], (((1,), (1,)), ((), ())),
                                         preferred_element_type=F32)
                scores = jnp.where(keep, scores, 0.0).astype(BF16)
                o = jnp.dot(scores, vbd_ref[dn, cc, p], preferred_element_type=F32)
                o = o + jnp.dot(qs_ref[dn, rows, pc], st16_ref[dn, p], preferred_element_type=F32)
                o_ref[0, rows, pc] = o.astype(BF16)
                v_stack = jnp.concatenate(
                    [v_ref[0, rows, p * HG_PAIR:p * HG_PAIR + HG_DK],
                     v_ref[0, rows, p * HG_PAIR + HG_DK:(p + 1) * HG_PAIR]], axis=0)
                kv = lax.dot_general(v_stack, kdbd_ref[dn, cc, p], (((0,), (0,)), ((), ())),
                                     preferred_element_type=F32)
                for hh in range(2):
                    sl = slice(hh * HG_DK, (hh + 1) * HG_DK)
                    cols = slice(p * HG_PAIR + hh * HG_DK, p * HG_PAIR + (hh + 1) * HG_DK)
                    new = st_ref[dn, p, sl, sl] * dec_ref[dn, cc:cc + 1, cols] + kv[:, sl]
                    st_ref[dn, p, sl, sl] = new
                    st16_ref[dn, p, sl, sl] = new.T.astype(BF16)

    @pl.when(i == pl.num_programs(1) - 1)
    def _():
        sfin_ref[0] = st_ref[...]


def _hg_scan(q, lf_f, lf_b, v, s0, casts=()):
    b, t, d = q.shape
    tb = _row_tile(t, HG_SCAN_BLOCK)
    nblk = t // tb
    fwd = pl.BlockSpec((1, tb, d), lambda bi, i: (bi, i, 0))
    bwd = pl.BlockSpec((1, tb, d), lambda bi, i: (bi, nblk - 1 - i, 0))
    st_spec = pl.BlockSpec((1,) + s0.shape[1:], lambda bi, i: (bi, 0, 0, 0, 0))
    blockdiag = pltpu.VMEM((2, tb // CHUNK, d // HG_PAIR, 2 * CHUNK, HG_PAIR), BF16)
    return _side_cast_call(
        _hg_scan_kernel, casts, b * nblk, lambda bi, i: bi * nblk + i,
        out_shape=(jax.ShapeDtypeStruct((b, t, d), BF16), jax.ShapeDtypeStruct((b, t, d), BF16),
                   jax.ShapeDtypeStruct(s0.shape, F32)),
        grid=(b, nblk),
        in_specs=[fwd, fwd, fwd, bwd, bwd, bwd, st_spec],
        out_specs=(fwd, bwd, st_spec),
        scratch_shapes=[pltpu.VMEM(s0.shape[1:], F32), pltpu.VMEM(s0.shape[1:], BF16),
                        pltpu.VMEM((2, tb, d), BF16), pltpu.VMEM((2, tb, d), BF16),
                        blockdiag, blockdiag, blockdiag, pltpu.VMEM((2, tb // CHUNK, d), F32)],
        compiler_params=_cparams("parallel", "arbitrary"),
        name="hg_scan",
        args=(q, lf_f, v, q, lf_b, v, s0))


def _lru_in_kernel(x_ref, g_ref, sh_ref, sc_ref, w_ref, gg_ref, xr_ref, *, ctx, n_lat):
    d = x_ref.shape[-1]
    sh = _mod_row(sh_ref, ctx, n_lat)
    sc = _mod_row(sc_ref, ctx, n_lat)
    h = (_rms(x_ref[0], g_ref[...] * (1.0 + sc)) + sh).astype(BF16)
    nc = 2 * MXU_N
    for c0 in range(0, 2 * d, nc):
        part, p0 = divmod(c0, d)
        z = jnp.dot(h, w_ref[:, c0:c0 + nc], preferred_element_type=F32)
        if part == 0:
            gg_ref[0, :, p0:p0 + nc] = _gelu_tanh(z).astype(BF16)
        else:
            xr_ref[0, :, p0:p0 + nc] = z


def _lru_in(x, mods, layer, norm_g, w_in, ctx, n_lat, casts=()):
    b, t, d = x.shape
    tm = _row_tile(t, IN_PROJ_ROW_TILE)
    nt = t // tm
    tile = pl.BlockSpec((1, tm, d), lambda bi, i: (bi, i, 0))
    return _side_cast_call(
        functools.partial(_lru_in_kernel, ctx=ctx, n_lat=n_lat),
        casts, b * nt, lambda bi, i: bi * nt + i,
        out_shape=(jax.ShapeDtypeStruct((b, t, d), BF16), jax.ShapeDtypeStruct((b, t, d), F32)),
        grid=(b, nt),
        in_specs=[tile, _resident((1, d)), _mod_spec(layer, 0, d), _mod_spec(layer, 1, d),
                  _resident(w_in.shape)],
        out_specs=(tile, tile),
        compiler_params=_cparams("parallel", "parallel"),
        name="lru_in",
        args=(x, norm_g.reshape(1, d), mods, mods, w_in))


LRU_COL_BLOCK = BF16_ROWS
LRU_GATE_ROWS = MXU_N
LRU_SCAN_SEGS = 2
LOG2_E = 1.4426950408889634
LN_2 = 0.6931471805599453
HALO = SUBLANES


def _softplus(x):
    y = jnp.exp(-jnp.abs(x))
    u = 1.0 + y
    log1p = jnp.where(u == 1.0, y, jnp.log(u) * (y / (u - 1.0)))
    return jnp.maximum(x, 0.0) + log1p


def _lru_scan_kernel(xf_ref, xfp_ref, xfn_ref, xb_ref, xbp_ref, xbn_ref, cw_ref, cb_ref, wa_ref, wx_ref,
                     ba_ref, bx_ref, lam_ref, h0_ref, hf_ref, hb_ref, hfin_ref, xpad_ref, a_ref, u_ref,
                     hl_ref, pr_ref, carry_ref):
    j = pl.program_id(2)
    nj = pl.num_programs(2)
    _, nr, wb, cb = xf_ref.shape

    @pl.when(j == 0)
    def _():
        carry_ref[...] = h0_ref[:, 0]

    col = lax.broadcasted_iota(jnp.int32, (wb, cb), 0)
    left = LRU_CONV_W - 1 - (LRU_CONV_W - 1) // 2
    right = LRU_CONV_W - 1 - left
    dirs = ((xf_ref, xfp_ref, xfn_ref, j), (xb_ref, xbp_ref, xbn_ref, nj - 1 - j))
    for dn, (x_ref, xp_ref, xn_ref, jb) in enumerate(dirs):
        for k in range(left):
            edge = jnp.where(jb > 0, xp_ref[0, HALO - left + k, wb - 1:wb, :], 0.0)
            xpad_ref[dn, k] = jnp.where(col == 0, edge, pltpu.roll(x_ref[0, nr - left + k], 1, axis=0))
        xpad_ref[dn, left:left + nr] = x_ref[0]
        for k in range(right):
            edge = jnp.where(jb < nj - 1, xn_ref[0, k, 0:1, :], 0.0)
            xpad_ref[dn, left + nr + k] = jnp.where(col == wb - 1, edge,
                                                    pltpu.roll(x_ref[0, k], wb - 1, axis=0))

    cw = [cw_ref[k:k + 1, :] for k in range(LRU_CONV_W)]
    cbias = cb_ref[...]
    k2 = [(-0.5 * LRU_C * LOG2_E) * _softplus(-lam_ref[dn]) for dn in range(2)]
    half_ba = [0.5 * ba_ref[dn] for dn in range(2)]
    half_bx = [0.5 * bx_ref[dn] for dn in range(2)]
    rc = LRU_GATE_ROWS // wb

    def gate_chunk(ci, carry):
        r0 = pl.multiple_of(ci * rc, rc)
        for dn in range(2):
            xc = cbias + sum(cw[k] * xpad_ref[dn, pl.ds(r0 + k, rc)] for k in range(LRU_CONV_W))
            xc = xc.reshape(rc * wb, cb)
            xcb = xc.astype(BF16)
            tr = jnp.tanh(jnp.dot(xcb, wa_ref[dn, 0], preferred_element_type=F32) + half_ba[dn])
            ti = jnp.tanh(jnp.dot(xcb, wx_ref[dn, 0], preferred_element_type=F32) + half_bx[dn])
            log2_a = k2[dn] * tr + k2[dn]
            a = jnp.exp2(log2_a)
            q2 = (a * a * 0.25 + 0.25) * jnp.tanh(log2_a * (-LN_2))
            half_mult = jnp.where(q2 > 0.0, q2 * lax.rsqrt(q2), 0.0)
            a_ref[dn, pl.ds(r0, rc)] = a.reshape(rc, wb, cb)
            u_ref[dn, pl.ds(r0, rc)] = (half_mult * (xc * ti + xc)).reshape(rc, wb, cb)
        return carry

    lax.fori_loop(0, nr // rc, gate_chunk, 0)

    n_seg = LRU_SCAN_SEGS
    seg = nr // n_seg

    def scan_step(rr, carry):
        out = []
        for dn in range(2):
            for s in range(n_seg):
                h, p = carry[dn * n_seg + s]
                r = s * seg + (rr if dn == 0 else seg - 1 - rr)
                a = a_ref[dn, r]
                h = a * h + u_ref[dn, r]
                p = a * p
                hl_ref[dn, r] = h
                pr_ref[dn, r] = p
                out.append((h, p))
        return tuple(out)

    zero = jnp.zeros((wb, cb), F32)
    one = jnp.ones((wb, cb), F32)
    ends = lax.fori_loop(0, seg, scan_step, ((zero, one),) * (2 * n_seg), unroll=2)

    for dn, o_ref in enumerate((hf_ref, hb_ref)):
        c_in = carry_ref[dn]
        c_units = [zero] * n_seg
        wls = range(wb) if dn == 0 else range(wb - 1, -1, -1)
        segs = range(n_seg) if dn == 0 else range(n_seg - 1, -1, -1)
        for wl in wls:
            for s in segs:
                h_end, p_end = ends[dn * n_seg + s]
                c_units[s] = jnp.where(col == wl, c_in, c_units[s])
                c_in = p_end[wl:wl + 1, :] * c_in + h_end[wl:wl + 1, :]
        carry_ref[dn] = c_in
        for s in range(n_seg):
            rows = slice(s * seg, (s + 1) * seg)
            o_ref[0, rows] = (hl_ref[dn, rows] + pr_ref[dn, rows] * c_units[s]).astype(o_ref.dtype)

    @pl.when(j == nj - 1)
    def _():
        hfin_ref[:, 0] = carry_ref[...]


def _lru_scan(xr, conv_w, conv_b, wa, wx, ba, bx, lam, h0, grid_w):
    b, t, c = xr.shape
    nr = t // grid_w
    cb = c // LRU_BLOCKS
    wb = LRU_COL_BLOCK
    nj = grid_w // wb
    x4 = xr.reshape(b, nr, grid_w, c)
    up = lambda j: j
    down = lambda j: nj - 1 - j
    main = lambda cj: pl.BlockSpec((1, nr, wb, cb), lambda bi, n, j: (bi, 0, cj(j), n))
    prev = lambda cj: pl.BlockSpec((1, HALO, wb, cb),
                                   lambda bi, n, j: (bi, nr // HALO - 1, jnp.maximum(cj(j) - 1, 0), n))
    nxt = lambda cj: pl.BlockSpec((1, HALO, wb, cb),
                                  lambda bi, n, j: (bi, 0, jnp.minimum(cj(j) + 1, nj - 1), n))
    chan = lambda rows: pl.BlockSpec((rows, cb), lambda bi, n, j: (0, n))
    per_dir = pl.BlockSpec((2, 1, cb), lambda bi, n, j: (0, 0, n))
    gate_w = pl.BlockSpec((2, 1, cb, cb), lambda bi, n, j: (0, n, 0, 0))
    state = pl.BlockSpec((2, 1, 1, cb), lambda bi, n, j: (0, bi, 0, n))
    hf, hb, hfin = pl.pallas_call(
        _lru_scan_kernel,
        out_shape=(jax.ShapeDtypeStruct(x4.shape, BF16), jax.ShapeDtypeStruct(x4.shape, BF16),
                   jax.ShapeDtypeStruct(h0.shape, F32)),
        grid=(b, LRU_BLOCKS, nj),
        in_specs=[main(up), prev(up), nxt(up), main(down), prev(down), nxt(down),
                  chan(LRU_CONV_W), chan(1), gate_w, gate_w, per_dir, per_dir, per_dir, state],
        out_specs=(main(up), main(down), state),
        scratch_shapes=[pltpu.VMEM((2, nr + LRU_CONV_W - 1, wb, cb), F32)]
        + [pltpu.VMEM((2, nr, wb, cb), F32)] * 4 + [pltpu.VMEM((2, 1, cb), F32)],
        compiler_params=_cparams("parallel", "parallel", "arbitrary"),
        name="lru_scan",
    )(x4, x4, x4, x4, x4, x4, conv_w, conv_b.reshape(1, c), wa, wx,
      ba.reshape(2, 1, c), bx.reshape(2, 1, c), lam.reshape(2, 1, c), h0)
    return hf.reshape(b, t, c), hb.reshape(b, t, c), hfin


FFN_ROW_TILE = 512


def _mix_ffn_kernel(af_ref, ab_ref, gs_ref, xm_ref, afp_ref, abp_ref, gsp_ref, xp_ref, afn_ref, abn_ref,
                    gsn_ref, xn_ref, gain_ref, wo_ref, gate1_ref, g_ref, sh_ref, sc_ref, gate_ref, wup_ref,
                    cw_ref, cb_ref, wdn_ref, fg_ref, out_ref, y_ref, lat_ref, h_ref, act_ref,
                    *, head_norm, ctx, n_lat, final_norm):
    i = pl.program_id(1)
    tm, d = xm_ref.shape[1], xm_ref.shape[2]
    dff = wdn_ref.shape[0]
    lo = slice(BF16_ROWS - HALO, BF16_ROWS)
    hi = slice(0, HALO)

    def readout_in(af, ab, gs):
        o = af.astype(F32) + ab.astype(F32)
        if head_norm:
            dv = gain_ref.shape[-1]
            o = jnp.concatenate([_rms(o[:, c0:c0 + dv], gain_ref[...]) for c0 in range(0, d, dv)], axis=-1)
        return o * gs.astype(F32)

    g = g_ref[...]
    sh = _mod_row(sh_ref, ctx, n_lat)
    sc = _mod_row(sc_ref, ctx, n_lat)
    g_sc = g * (1.0 + sc)
    mod = lambda x: _rms(x, g_sc) + sh
    rows = tm + 2 * HALO
    y_ref[...] = jnp.concatenate(
        [readout_in(afp_ref[0, lo], abp_ref[0, lo], gsp_ref[0, lo]),
         readout_in(af_ref[0], ab_ref[0], gs_ref[0]),
         readout_in(afn_ref[0, hi], abn_ref[0, hi], gsn_ref[0, hi])], axis=0).astype(BF16)
    x_all = jnp.concatenate([xp_ref[0, lo], xm_ref[0], xn_ref[0, hi]], axis=0)
    lat_ref[...] = x_all + _mod_row(gate1_ref, ctx, n_lat) * jnp.dot(y_ref[...], wo_ref[...],
                                                                     preferred_element_type=F32)
    hp = jnp.where(i > 0, mod(lat_ref[0:HALO]), 0.0)
    hn = jnp.where(i < pl.num_programs(1) - 1, mod(lat_ref[HALO + tm:]), 0.0)
    h_ref[...] = jnp.concatenate([hp, mod(lat_ref[HALO:HALO + tm]), hn], axis=0).astype(BF16)
    nc = MXU_N
    for c0 in range(0, dff, nc):
        halves = []
        for base in (c0, dff + c0):
            u = jnp.dot(h_ref[...], wup_ref[:, base:base + nc], preferred_element_type=F32)
            cols = slice(base, base + nc)
            conv = (cb_ref[:, cols]
                    + cw_ref[0:1, cols] * pltpu.roll(u, 1, axis=0)[HALO:HALO + tm]
                    + cw_ref[1:2, cols] * u[HALO:HALO + tm]
                    + cw_ref[2:3, cols] * pltpu.roll(u, rows - 1, axis=0)[HALO:HALO + tm])
            halves.append(conv)
        act_ref[:, c0:c0 + nc] = (_silu(halves[0]) * halves[1]).astype(BF16)
    half = tm // 2
    for r0 in (0, half):
        z = jnp.dot(act_ref[r0:r0 + half], wdn_ref[...], preferred_element_type=F32)
        y = lat_ref[HALO + r0:HALO + r0 + half] + _mod_row(gate_ref, ctx, n_lat) * z
        out_ref[0, r0:r0 + half] = _rms(y, fg_ref[...]) if final_norm else y


def _mix_ffn(a_f, a_b, gs, x, mods, layer, head_gain, w_out, norm_g, w_up, conv_w, conv_b, w_down, final_g,
             ctx, n_lat, final_norm, casts=()):
    b, t, d = x.shape
    dff = w_down.shape[0]
    tm = _row_tile(t, FFN_ROW_TILE)
    nt = t // tm
    per = tm // BF16_ROWS
    tile = pl.BlockSpec((1, tm, d), lambda bi, i: (bi, i, 0))
    prev = pl.BlockSpec((1, BF16_ROWS, d), lambda bi, i: (bi, jnp.maximum(i * per - 1, 0), 0))
    nxt = pl.BlockSpec((1, BF16_ROWS, d), lambda bi, i: (bi, jnp.minimum((i + 1) * per, t // BF16_ROWS - 1), 0))
    head_norm = head_gain is not None
    gain = head_gain.reshape(1, -1) if head_norm else jnp.ones((1, LANES), F32)
    rows = tm + 2 * HALO
    (out,), cast_w = _side_cast_call(
        functools.partial(_mix_ffn_kernel, head_norm=head_norm, ctx=ctx, n_lat=n_lat, final_norm=final_norm),
        casts, b * nt, lambda bi, i: bi * nt + i,
        out_shape=(jax.ShapeDtypeStruct((b, t, d), F32),),
        grid=(b, nt),
        in_specs=[tile, tile, tile, tile, prev, prev, prev, prev, nxt, nxt, nxt, nxt,
                  _resident(gain.shape), _resident(w_out.shape), _mod_spec(layer, 2, d),
                  _resident((1, d)), _mod_spec(layer, 3, d), _mod_spec(layer, 4, d), _mod_spec(layer, 5, d),
                  _resident(w_up.shape), _resident(conv_w.shape), _resident((1, 2 * dff)),
                  _resident(w_down.shape), _resident((1, d))],
        out_specs=(tile,),
        scratch_shapes=[pltpu.VMEM((rows, d), BF16), pltpu.VMEM((rows, d), F32), pltpu.VMEM((rows, d), BF16),
                        pltpu.VMEM((tm, dff), BF16)],
        compiler_params=_cparams("parallel", "parallel"),
        name="mix_ffn",
        args=(a_f, a_b, gs, x, a_f, a_b, gs, x, a_f, a_b, gs, x, gain, w_out, mods,
              norm_g.reshape(1, d), mods, mods, mods, w_up, conv_w, conv_b.reshape(1, 2 * dff), w_down,
              final_g.reshape(1, d)))
    return out, cast_w


def kernel(x, c, ctx, c_ctx, w_ada, b_ada, norm_g, hg_w_in, hg_lb_logits, hg_gnorm, hg_w_out, lru_w_in,
           lru_conv_w, lru_conv_b, lru_wa, lru_ba, lru_wx, lru_bx, lru_lambda, lru_w_out, ffn_w_up,
           ffn_conv_w, ffn_conv_b, ffn_w_down, final_g):
    nb, _, d = x.shape
    depth = w_ada.shape[0]
    n_mixers = 2

    weights = {"hg_in": (hg_w_in, 1.0), "hg_out": (hg_w_out, 1.0), "lru_in": (lru_w_in, 1.0),
               "lru_out": (lru_w_out, 1.0), "lru_wa": (lru_wa, 0.5), "lru_wx": (lru_wx, 0.5),
               "ffn_up": (ffn_w_up, 1.0), "ffn_down": (ffn_w_down, 1.0)}
    ready = {}

    def bf(name, idx):
        if (name, idx) not in ready:
            ready[name, idx] = _layer_bf16(weights[name][0], idx, weights[name][1])
        return ready[name, idx]

    def hosted(call, *keys):
        keys = [k for k in keys if k not in ready]
        outs, cast_w = call(casts=[_SideCast(weights[n][0], i, weights[n][1]) for n, i in keys])
        ready.update(zip(keys, cast_w))
        return outs

    def mixer_keys(l):
        j = l // n_mixers
        names = ("hg_in", "hg_out") if l % n_mixers == 0 else ("lru_in", "lru_out", "lru_wa", "lru_wx")
        return [(n, j) for n in names]

    cond = jnp.concatenate([c, c_ctx[None, :], jnp.zeros((COND_ROWS - nb - 1, d), F32)], axis=0)
    mods = _ada(cond.T, w_ada, b_ada, nb + 1)

    lat, cx = x, ctx
    for l in range(depth):
        last = l == depth - 1
        j = l // n_mixers
        if l % n_mixers == 0:
            w_in, w_out = bf("hg_in", j), bf("hg_out", j)
            s0 = jnp.zeros((nb, 2, d // HG_PAIR, HG_PAIR, HG_PAIR), F32)
            lb_logits = jnp.swapaxes(hg_lb_logits, 0, 1)
            hg_in = functools.partial(_hg_in, mods=mods, layer=l, norm_g=norm_g[l, 0], w_in=w_in,
                                      lb_logits=lb_logits, layer_j=j, n_lat=nb)
            pc, _ = hg_in(cx, ctx=True)
            (oc_f, oc_b, s_ctx), _ = _hg_scan(pc[0], pc[1], pc[2], pc[3], s0)
            pl_ = hosted(functools.partial(hg_in, lat, ctx=False), ("ffn_down", l))
            ol_f, ol_b, _ = hosted(functools.partial(_hg_scan, pl_[0], pl_[1], pl_[2], pl_[3], s_ctx),
                                   ("ffn_up", l))
            mix_lat = (ol_f, ol_b, pl_[4])
            mix_ctx = (oc_f, oc_b, pc[4])
            head_gain = hg_gnorm[j]
        else:
            w_in, w_out = bf("lru_in", j), bf("lru_out", j)
            scan = functools.partial(_lru_scan, conv_w=lru_conv_w[j], conv_b=lru_conv_b[j],
                                     wa=bf("lru_wa", j), wx=bf("lru_wx", j),
                                     ba=lru_ba[j], bx=lru_bx[j], lam=lru_lambda[j])
            lru_in = functools.partial(_lru_in, mods=mods, layer=l, norm_g=norm_g[l, 0], w_in=w_in, n_lat=nb)
            wc = LRU_COL_BLOCK
            t_ctx = cx.shape[1]
            flip = lambda a: jnp.swapaxes(a.reshape(nb, wc, t_ctx // wc, d), 1, 2).reshape(nb, t_ctx, d)
            unflip = lambda a: jnp.swapaxes(a.reshape(nb, t_ctx // wc, wc, d), 1, 2).reshape(nb, t_ctx, d)
            (gg_c, xr_c), _ = lru_in(cx, ctx=True)
            hc_f, hc_b, h_ctx = scan(flip(xr_c), h0=jnp.zeros((2, nb, 1, d), F32), grid_w=wc)
            gg_l, xr_l = hosted(functools.partial(lru_in, lat, ctx=False), ("ffn_down", l))
            hl_f, hl_b, _ = scan(xr_l, h0=h_ctx, grid_w=GRID_W)
            mix_lat = (hl_f, hl_b, gg_l)
            mix_ctx = (hc_f, hc_b, gg_c)
            head_gain = None
        mix_ffn = functools.partial(_mix_ffn, mods=mods, layer=l, head_gain=head_gain, w_out=w_out,
                                    norm_g=norm_g[l, 1], w_up=bf("ffn_up", l), conv_w=ffn_conv_w[l],
                                    conv_b=ffn_conv_b[l], w_down=bf("ffn_down", l), final_g=final_g, n_lat=nb)
        nxt = [] if last else mixer_keys(l + 1) + ([("ffn_up", l + 1)] if (l + 1) % n_mixers else [])
        lat = hosted(functools.partial(mix_ffn, *mix_lat, lat, ctx=False, final_norm=last), *nxt)
        if not last:
            if head_gain is None:
                mix_ctx = (unflip(mix_ctx[0]), unflip(mix_ctx[1]), mix_ctx[2])
            cx, _ = mix_ffn(*mix_ctx, cx, ctx=True, final_norm=False)
    return lat
```

```python
import functools

import jax
import jax.numpy as jnp
from jax import lax
from jax.experimental import pallas as pl
from jax.experimental.pallas import tpu as pltpu

F32 = jnp.float32
BF16 = jnp.bfloat16

EPS = 1e-6
GRID_W = 64
HG_DK = 128
CHUNK = 64
LRU_BLOCKS = 4
LRU_C = 8.0
LRU_CONV_W = 4
FFN_CONV_W = 3
N_MOD = 6

SUBLANES = 8
LANES = 128
BF16_ROWS = 16
MXU_N = 256
VMEM_LIMIT_BYTES = 56 * 1024 * 1024
COND_ROWS = SUBLANES


def _cparams(*sem):
    return pltpu.CompilerParams(dimension_semantics=sem, vmem_limit_bytes=VMEM_LIMIT_BYTES)


def _resident(shape):
    zeros = (0,) * len(shape)
    return pl.BlockSpec(shape, lambda *_: zeros, pipeline_mode=pl.Buffered(1))


IN_PROJ_ROW_TILE = 1024


def _row_tile(t, largest):
    tm = largest
    while t % tm:
        tm //= 2
    return tm


def _sigmoid(x):
    return 0.5 + 0.5 * jnp.tanh(0.5 * x)


def _silu(x):
    return x * _sigmoid(x)


def _gelu_tanh(x):
    return 0.5 * x * (1.0 + jnp.tanh(0.7978845608028654 * (x + 0.044715 * (x * x * x))))


def _rms(x, g):
    return x * lax.rsqrt(jnp.mean(x * x, axis=-1, keepdims=True) + EPS) * g


def _mod_row(mod_ref, ctx, n_lat):
    if ctx:
        return mod_ref[0, n_lat:n_lat + 1, :]
    return mod_ref[0, pl.ds(pl.program_id(0), 1), :]


def _mod_spec(layer, k, d):
    return pl.BlockSpec((1, COND_ROWS, d), lambda *_: (layer, 0, k))


CAST_BLOCK_BYTES = 4 * 1024 * 1024


def _cast_kernel(w_ref, o_ref, *, scale):
    w = w_ref[0]
    o_ref[...] = (w if scale == 1.0 else w * scale).astype(o_ref.dtype)


def _layer_bf16(w, layer, scale=1.0):
    shape = w.shape[1:]
    cols = shape[-1]
    rows = w[0].size // cols
    tr = rows
    while tr * cols * 4 > CAST_BLOCK_BYTES and tr % (2 * BF16_ROWS) == 0:
        tr //= 2
    out = pl.pallas_call(
        functools.partial(_cast_kernel, scale=scale),
        out_shape=jax.ShapeDtypeStruct((rows, cols), BF16),
        grid=(rows // tr,),
        in_specs=[pl.BlockSpec((1, tr, cols), lambda i: (layer, i, 0))],
        out_specs=pl.BlockSpec((tr, cols), lambda i: (i, 0)),
        compiler_params=_cparams("parallel"),
        name="cast_bf16",
    )(w.reshape(w.shape[0], rows, cols))
    return out.reshape(shape)


class _SideCast:
    def __init__(self, w, layer, scale=1.0):
        self.shape = w.shape[1:]
        self.cols = w.shape[-1]
        self.rows = w[0].size // self.cols
        self.w = w.reshape(w.shape[0], self.rows, self.cols)
        self.layer = layer
        self.scale = scale

    def specs(self, n_steps, step):
        slab, rem = divmod(self.rows, n_steps)
        assert rem == 0 and slab % BF16_ROWS == 0, (self.rows, n_steps)
        layer = self.layer
        return (pl.BlockSpec((1, slab, self.cols), lambda *ids: (layer, step(*ids), 0)),
                pl.BlockSpec((slab, self.cols), lambda *ids: (step(*ids), 0)),
                jax.ShapeDtypeStruct((self.rows, self.cols), BF16))


def _with_side_casts(body, n_in, n_out, casts):
    n_c = len(casts)

    def kernel(*refs):
        ins, c_in = refs[:n_in], refs[n_in:n_in + n_c]
        outs = refs[n_in + n_c:n_in + n_c + n_out]
        c_out = refs[n_in + n_c + n_out:n_in + 2 * n_c + n_out]
        for w_ref, o_ref, cast in zip(c_in, c_out, casts):
            w = w_ref[0]
            o_ref[...] = (w if cast.scale == 1.0 else w * cast.scale).astype(o_ref.dtype)
        body(*ins, *outs, *refs[n_in + 2 * n_c + n_out:])

    return kernel


def _side_cast_call(body, casts, n_steps, step, *, out_shape, in_specs, out_specs, args, **kw):
    spec3 = [c.specs(n_steps, step) for c in casts]
    outs = pl.pallas_call(
        _with_side_casts(body, len(in_specs), len(out_shape), casts),
        out_shape=tuple(out_shape) + tuple(s[2] for s in spec3),
        in_specs=list(in_specs) + [s[0] for s in spec3],
        out_specs=tuple(out_specs) + tuple(s[1] for s in spec3),
        **kw,
    )(*args, *[c.w for c in casts])
    n = len(out_shape)
    return outs[:n], [o.reshape(c.shape) for o, c in zip(outs[n:], casts)]


def _ada_kernel(ct_ref, w_ref, b_ref, o_ref, *, n_cond):
    ct = ct_ref[...]
    s = _silu(ct)
    w = w_ref[0]
    o_ref[0] = jnp.zeros(o_ref.shape[1:], F32) + b_ref[0]
    for m in range(n_cond):
        o_ref[0, m:m + 1, :] = jnp.sum(w * s[:, m:m + 1], axis=0, keepdims=True) + b_ref[0]


def _ada(cond_t, w_ada, b_ada, n_cond):
    depth, d, n = w_ada.shape
    tn = 1024
    return pl.pallas_call(
        functools.partial(_ada_kernel, n_cond=n_cond),
        out_shape=jax.ShapeDtypeStruct((depth, COND_ROWS, n), F32),
        grid=(depth, n // tn),
        in_specs=[
            pl.BlockSpec((d, COND_ROWS), lambda l, j: (0, 0)),
            pl.BlockSpec((1, d, tn), lambda l, j: (l, 0, j)),
            pl.BlockSpec((1, 1, tn), lambda l, j: (l, 0, j)),
        ],
        out_specs=pl.BlockSpec((1, COND_ROWS, tn), lambda l, j: (l, 0, j)),
        compiler_params=_cparams("parallel", "parallel"),
        name="ada",
    )(cond_t, w_ada, b_ada.reshape(depth, 1, n))


def _split2(x):
    hi = x.astype(BF16)
    return hi, (x - hi.astype(F32)).astype(BF16)


def _hg_in_kernel(x_ref, g_ref, sh_ref, sc_ref, w_ref, lbl_ref, q_ref, lff_ref, lfb_ref, v_ref, gs_ref,
                  *, ctx, n_lat, layer_j):
    d = x_ref.shape[-1]
    sh = _mod_row(sh_ref, ctx, n_lat)
    sc = _mod_row(sc_ref, ctx, n_lat)
    h = (_rms(x_ref[0], g_ref[...] * (1.0 + sc)) + sh).astype(BF16)
    nc = 2 * MXU_N
    for c0 in range(0, 5 * d, nc):
        part, p0 = divmod(c0, d)
        z = jnp.dot(h, w_ref[:, c0:c0 + nc], preferred_element_type=F32)
        cols = slice(p0, p0 + nc)
        if part == 0:
            q_ref[0, :, cols] = _silu(z).astype(BF16)
        elif part in (1, 2):
            lg = lbl_ref[part - 1, :, cols]
            e = jnp.exp(lg - jnp.max(lg, axis=0, keepdims=True))
            lb = jnp.sum(e[:layer_j + 1], axis=0, keepdims=True) / jnp.sum(e, axis=0, keepdims=True)
            f = lb + (1.0 - lb) * _sigmoid(z)
            (lff_ref if part == 1 else lfb_ref)[0, :, cols] = jnp.log(f)
        elif part == 3:
            v_ref[0, :, cols] = z.astype(BF16)
        else:
            gs_ref[0, :, cols] = _silu(z).astype(BF16)


def _hg_in(x, mods, layer, norm_g, w_in, lb_logits, layer_j, ctx, n_lat, casts=()):
    b, t, d = x.shape
    tm = _row_tile(t, IN_PROJ_ROW_TILE)
    nt = t // tm
    tile = pl.BlockSpec((1, tm, d), lambda bi, i: (bi, i, 0))
    sds = lambda dt: jax.ShapeDtypeStruct((b, t, d), dt)
    return _side_cast_call(
        functools.partial(_hg_in_kernel, ctx=ctx, n_lat=n_lat, layer_j=layer_j),
        casts, b * nt, lambda bi, i: bi * nt + i,
        out_shape=(sds(BF16), sds(F32), sds(F32), sds(BF16), sds(BF16)),
        grid=(b, nt),
        in_specs=[tile, _resident((1, d)), _mod_spec(layer, 0, d), _mod_spec(layer, 1, d),
                  _resident(w_in.shape), _resident(lb_logits.shape)],
        out_specs=(tile, tile, tile, tile, tile),
        compiler_params=_cparams("parallel", "parallel"),
        name="hg_in",
        args=(x, norm_g.reshape(1, d), mods, mods, w_in, lb_logits))


HG_PAIR = 2 * HG_DK
HG_SCAN_BLOCK = 512


def _hg_scan_kernel(qf_ref, lf_ref, vf_ref, qb_ref, lb_ref, vb_ref, s0_ref, of_ref, ob_ref, sfin_ref,
                    st_ref, st16_ref, qe_ref, qs_ref, kebd_ref, kdbd_ref, vbd_ref, dec_ref):
    i = pl.program_id(1)
    tb, d = qf_ref.shape[1], qf_ref.shape[2]
    n_chunks = tb // CHUNK
    n_pairs = d // HG_PAIR

    @pl.when(i == 0)
    def _():
        st_ref[...] = s0_ref[0]
        for dn in range(2):
            for p in range(n_pairs):
                st16_ref[dn, p] = s0_ref[0, dn, p].T.astype(BF16)
        kebd_ref[...] = jnp.zeros(kebd_ref.shape, BF16)
        kdbd_ref[...] = jnp.zeros(kdbd_ref.shape, BF16)
        vbd_ref[...] = jnp.zeros(vbd_ref.shape, BF16)

    grp = min(tb, MXU_N)
    row = lax.broadcasted_iota(jnp.int32, (grp, grp), 0)
    col = lax.broadcasted_iota(jnp.int32, (grp, grp), 1)
    same_chunk = (row // CHUNK) == (col // CHUNK)
    r_in = lax.broadcasted_iota(jnp.int32, (CHUNK, 2 * CHUNK), 0)
    c_in = lax.broadcasted_iota(jnp.int32, (CHUNK, 2 * CHUNK), 1) % CHUNK
    dirs = (
        (qf_ref, lf_ref, vf_ref, of_ref, same_chunk & (row >= col), r_in >= c_in, CHUNK // 2, CHUNK - 1),
        (qb_ref, lb_ref, vb_ref, ob_ref, same_chunk & (row <= col), r_in <= c_in, CHUNK // 2 - 1, 0),
    )

    for dn, (q_ref, lfd_ref, v_ref, _, keep_blk, _, ref_row, last_row) in enumerate(dirs):
        tri = jnp.where(keep_blk, 1.0, 0.0).astype(BF16)
        lf = lfd_ref[0]
        b = jnp.concatenate(
            [sum(jnp.dot(tri, part, preferred_element_type=F32) for part in _split2(lf[r0:r0 + grp]))
             for r0 in range(0, tb, grp)], axis=0)
        for c in range(n_chunks):
            rows = slice(c * CHUNK, (c + 1) * CHUNK)
            bc = b[rows]
            b_mid = bc[ref_row:ref_row + 1, :]
            b_end = bc[last_row:last_row + 1, :]
            e_q = jnp.exp(bc - b_mid)
            qe = q_ref[0, rows, :].astype(F32) * e_q
            ke = (1.0 - jnp.exp(lf[rows])) * (1.0 / e_q)
            qe_ref[dn, rows, :] = qe.astype(BF16)
            qs_ref[dn, rows, :] = (qe * jnp.exp(b_mid)).astype(BF16)
            ke16 = ke.astype(BF16)
            kd16 = (ke * jnp.exp(b_end - b_mid)).astype(BF16)
            for h in range(2 * n_pairs):
                p, hh = divmod(h, 2)
                cols = slice(h * HG_DK, (h + 1) * HG_DK)
                blk = (slice(hh * CHUNK, (hh + 1) * CHUNK), slice(hh * HG_DK, (hh + 1) * HG_DK))
                kebd_ref[dn, c, p, blk[0], blk[1]] = ke16[:, cols]
                kdbd_ref[dn, c, p, blk[0], blk[1]] = kd16[:, cols]
                vbd_ref[dn, c, p, blk[0], blk[1]] = v_ref[0, rows, cols]
            dec_ref[dn, c:c + 1, :] = jnp.exp(b_end)

    for c in range(n_chunks):
        for dn, (_, _, v_ref, o_ref, _, keep, _, _) in enumerate(dirs):
            cc = c if dn == 0 else n_chunks - 1 - c
            rows = slice(cc * CHUNK, (cc + 1) * CHUNK)
            for p in range(n_pairs):
                pc = slice(p * HG_PAIR, (p + 1) * HG_PAIR)
                scores = lax.dot_general(qe_ref[dn, rows, pc], kebd_ref[dn, cc, p], (((1,), (1,)), ((), ())),
                                         preferred_element_type=F32)
                scores = jnp.where(keep, scores, 0.0).astype(BF16)
                o = jnp.dot(scores, vbd_ref[dn, cc, p], preferred_element_type=F32)
                o = o + jnp.dot(qs_ref[dn, rows, pc], st16_ref[dn, p], preferred_element_type=F32)
                o_ref[0, rows, pc] = o.astype(BF16)
                v_stack = jnp.concatenate(
                    [v_ref[0, rows, p * HG_PAIR:p * HG_PAIR + HG_DK],
                     v_ref[0, rows, p * HG_PAIR + HG_DK:(p + 1) * HG_PAIR]], axis=0)
                kv = lax.dot_general(v_stack, kdbd_ref[dn, cc, p], (((0,), (0,)), ((), ())),
                                     preferred_element_type=F32)
                for hh in range(2):
                    sl = slice(hh * HG_DK, (hh + 1) * HG_DK)
                    cols = slice(p * HG_PAIR + hh * HG_DK, p * HG_PAIR + (hh + 1) * HG_DK)
                    new = st_ref[dn, p, sl, sl] * dec_ref[dn, cc:cc + 1, cols] + kv[:, sl]
                    st_ref[dn, p, sl, sl] = new
                    st16_ref[dn, p, sl, sl] = new.T.astype(BF16)

    @pl.when(i == pl.num_programs(1) - 1)
    def _():
        sfin_ref[0] = st_ref[...]


def _hg_scan(q, lf_f, lf_b, v, s0, casts=()):
    b, t, d = q.shape
    tb = _row_tile(t, HG_SCAN_BLOCK)
    nblk = t // tb
    fwd = pl.BlockSpec((1, tb, d), lambda bi, i: (bi, i, 0))
    bwd = pl.BlockSpec((1, tb, d), lambda bi, i: (bi, nblk - 1 - i, 0))
    st_spec = pl.BlockSpec((1,) + s0.shape[1:], lambda bi, i: (bi, 0, 0, 0, 0))
    blockdiag = pltpu.VMEM((2, tb // CHUNK, d // HG_PAIR, 2 * CHUNK, HG_PAIR), BF16)
    return _side_cast_call(
        _hg_scan_kernel, casts, b * nblk, lambda bi, i: bi * nblk + i,
        out_shape=(jax.ShapeDtypeStruct((b, t, d), BF16), jax.ShapeDtypeStruct((b, t, d), BF16),
                   jax.ShapeDtypeStruct(s0.shape, F32)),
        grid=(b, nblk),
        in_specs=[fwd, fwd, fwd, bwd, bwd, bwd, st_spec],
        out_specs=(fwd, bwd, st_spec),
        scratch_shapes=[pltpu.VMEM(s0.shape[1:], F32), pltpu.VMEM(s0.shape[1:], BF16),
                        pltpu.VMEM((2, tb, d), BF16), pltpu.VMEM((2, tb, d), BF16),
                        blockdiag, blockdiag, blockdiag, pltpu.VMEM((2, tb // CHUNK, d), F32)],
        compiler_params=_cparams("parallel", "arbitrary"),
        name="hg_scan",
        args=(q, lf_f, v, q, lf_b, v, s0))


def _lru_in_kernel(x_ref, g_ref, sh_ref, sc_ref, w_ref, gg_ref, xr_ref, *, ctx, n_lat):
    d = x_ref.shape[-1]
    sh = _mod_row(sh_ref, ctx, n_lat)
    sc = _mod_row(sc_ref, ctx, n_lat)
    h = (_rms(x_ref[0], g_ref[...] * (1.0 + sc)) + sh).astype(BF16)
    nc = 2 * MXU_N
    for c0 in range(0, 2 * d, nc):
        part, p0 = divmod(c0, d)
        z = jnp.dot(h, w_ref[:, c0:c0 + nc], preferred_element_type=F32)
        if part == 0:
            gg_ref[0, :, p0:p0 + nc] = _gelu_tanh(z).astype(BF16)
        else:
            xr_ref[0, :, p0:p0 + nc] = z


def _lru_in(x, mods, layer, norm_g, w_in, ctx, n_lat, casts=()):
    b, t, d = x.shape
    tm = _row_tile(t, IN_PROJ_ROW_TILE)
    nt = t // tm
    tile = pl.BlockSpec((1, tm, d), lambda bi, i: (bi, i, 0))
    return _side_cast_call(
        functools.partial(_lru_in_kernel, ctx=ctx, n_lat=n_lat),
        casts, b * nt, lambda bi, i: bi * nt + i,
        out_shape=(jax.ShapeDtypeStruct((b, t, d), BF16), jax.ShapeDtypeStruct((b, t, d), F32)),
        grid=(b, nt),
        in_specs=[tile, _resident((1, d)), _mod_spec(layer, 0, d), _mod_spec(layer, 1, d),
                  _resident(w_in.shape)],
        out_specs=(tile, tile),
        compiler_params=_cparams("parallel", "parallel"),
        name="lru_in",
        args=(x, norm_g.reshape(1, d), mods, mods, w_in))


LRU_COL_BLOCK = BF16_ROWS
LRU_GATE_ROWS = MXU_N
LRU_SCAN_SEGS = 2
LRU_SCAN_UNROLL = 8
LOG2_E = 1.4426950408889634
LN_2 = 0.6931471805599453
HALO = SUBLANES


def _softplus(x):
    y = jnp.exp(-jnp.abs(x))
    u = 1.0 + y
    log1p = jnp.where(u == 1.0, y, jnp.log(u) * (y / (u - 1.0)))
    return jnp.maximum(x, 0.0) + log1p


def _lru_scan_kernel(xf_ref, xfp_ref, xfn_ref, xb_ref, xbp_ref, xbn_ref, cw_ref, cb_ref, wa_ref, wx_ref,
                     ba_ref, bx_ref, lam_ref, h0_ref, hf_ref, hb_ref, hfin_ref, xpad_ref, a_ref, u_ref,
                     carry_ref):
    j = pl.program_id(2)
    nj = pl.num_programs(2)
    _, nr, wb, cb = xf_ref.shape

    @pl.when(j == 0)
    def _():
        carry_ref[...] = h0_ref[:, 0]

    col = lax.broadcasted_iota(jnp.int32, (wb, cb), 0)
    left = LRU_CONV_W - 1 - (LRU_CONV_W - 1) // 2
    right = LRU_CONV_W - 1 - left
    rc = LRU_GATE_ROWS // wb
    n_ch = nr // rc
    x_refs = (xf_ref, xb_ref)
    for dn, (x_ref, xp_ref, xn_ref, jb) in enumerate(((xf_ref, xfp_ref, xfn_ref, j),
                                                       (xb_ref, xbp_ref, xbn_ref, nj - 1 - j))):
        last_end = min(n_ch - 1, 1)
        for k in range(left):
            edge = jnp.where(jb > 0, xp_ref[0, HALO - left + k, wb - 1:wb, :], 0.0)
            xpad_ref[dn, 0, k] = jnp.where(col == 0, edge, pltpu.roll(x_ref[0, nr - left + k], 1, axis=0))
        for k in range(right):
            edge = jnp.where(jb < nj - 1, xn_ref[0, k, 0:1, :], 0.0)
            xpad_ref[dn, last_end, left + rc + k] = jnp.where(col == wb - 1, edge,
                                                              pltpu.roll(x_ref[0, k], wb - 1, axis=0))
        if n_ch == 1:
            xpad_ref[dn, 0, left:left + rc] = x_ref[0]
        else:
            xpad_ref[dn, 0, left:] = x_ref[0, 0:rc + right]
            xpad_ref[dn, 1, 0:left + rc] = x_ref[0, nr - rc - left:nr]

    cw = [cw_ref[k:k + 1, :] for k in range(LRU_CONV_W)]
    cbias = cb_ref[...]
    k2 = [(-0.5 * LRU_C * LOG2_E) * _softplus(-lam_ref[dn]) for dn in range(2)]
    half_ba = [0.5 * ba_ref[dn] for dn in range(2)]
    half_bx = [0.5 * bx_ref[dn] for dn in range(2)]

    def gates(dn, rows, tap):
        xc = cbias + sum(cw[k] * tap(k) for k in range(LRU_CONV_W))
        xc = xc.reshape(rc * wb, cb)
        xcb = xc.astype(BF16)
        tr = jnp.tanh(jnp.dot(xcb, wa_ref[dn, 0], preferred_element_type=F32) + half_ba[dn])
        ti = jnp.tanh(jnp.dot(xcb, wx_ref[dn, 0], preferred_element_type=F32) + half_bx[dn])
        log2_a = k2[dn] * tr + k2[dn]
        a = jnp.exp2(log2_a)
        q2 = (a * a * 0.25 + 0.25) * jnp.tanh(log2_a * (-LN_2))
        half_mult = jnp.where(q2 > 0.0, q2 * lax.rsqrt(q2), 0.0)
        a_ref[dn, rows] = a.reshape(rc, wb, cb)
        u_ref[dn, rows] = (half_mult * (xc * ti + xc)).reshape(rc, wb, cb)

    for end, r0 in ((0, 0), (1, nr - rc))[:min(n_ch, 2)]:
        for dn in range(2):
            gates(dn, slice(r0, r0 + rc), lambda k, dn=dn, end=end: xpad_ref[dn, end, k:k + rc])

    def interior_chunk(ci, carry):
        r0 = pl.multiple_of(ci * rc, rc)
        for dn in range(2):
            gates(dn, pl.ds(r0, rc), lambda k, dn=dn: x_refs[dn][0, pl.ds(r0 - left + k, rc)])
        return carry

    if n_ch > 2:
        lax.fori_loop(1, n_ch - 1, interior_chunk, 0)

    n_seg = LRU_SCAN_SEGS
    seg = nr // n_seg
    unroll = min(seg, LRU_SCAN_UNROLL)
    row_of = lambda dn, s, rr: s * seg + (rr if dn == 0 else seg - 1 - rr)
    units = [(dn, s) for dn in range(2) for s in range(n_seg)]

    def sweep1(rr, carry):
        out = []
        for (dn, s), (h, p) in zip(units, carry):
            a = a_ref[dn, row_of(dn, s, rr)]
            out.append((a * h + u_ref[dn, row_of(dn, s, rr)], a * p))
        return tuple(out)

    zero = jnp.zeros((wb, cb), F32)
    one = jnp.ones((wb, cb), F32)
    ends = lax.fori_loop(0, seg, sweep1, ((zero, one),) * len(units), unroll=unroll)

    entry = {}
    for dn in range(2):
        c_in = carry_ref[dn]
        wls = range(wb) if dn == 0 else range(wb - 1, -1, -1)
        segs = range(n_seg) if dn == 0 else range(n_seg - 1, -1, -1)
        for s in segs:
            entry[dn, s] = zero
        for wl in wls:
            for s in segs:
                h_end, p_end = ends[units.index((dn, s))]
                entry[dn, s] = jnp.where(col == wl, c_in, entry[dn, s])
                c_in = p_end[wl:wl + 1, :] * c_in + h_end[wl:wl + 1, :]
        carry_ref[dn] = c_in

    o_refs = (hf_ref, hb_ref)

    def sweep2(rr, hs):
        out = []
        for (dn, s), h in zip(units, hs):
            r = row_of(dn, s, rr)
            h = a_ref[dn, r] * h + u_ref[dn, r]
            o_refs[dn][0, r] = h.astype(o_refs[dn].dtype)
            out.append(h)
        return tuple(out)

    lax.fori_loop(0, seg, sweep2, tuple(entry[u] for u in units), unroll=unroll)

    @pl.when(j == nj - 1)
    def _():
        hfin_ref[:, 0] = carry_ref[...]


def _lru_scan(xr, conv_w, conv_b, wa, wx, ba, bx, lam, h0, grid_w):
    b, t, c = xr.shape
    nr = t // grid_w
    cb = c // LRU_BLOCKS
    wb = LRU_COL_BLOCK
    nj = grid_w // wb
    x4 = xr.reshape(b, nr, grid_w, c)
    up = lambda j: j
    down = lambda j: nj - 1 - j
    main = lambda cj: pl.BlockSpec((1, nr, wb, cb), lambda bi, n, j: (bi, 0, cj(j), n))
    prev = lambda cj: pl.BlockSpec((1, HALO, wb, cb),
                                   lambda bi, n, j: (bi, nr // HALO - 1, jnp.maximum(cj(j) - 1, 0), n))
    nxt = lambda cj: pl.BlockSpec((1, HALO, wb, cb),
                                  lambda bi, n, j: (bi, 0, jnp.minimum(cj(j) + 1, nj - 1), n))
    chan = lambda rows: pl.BlockSpec((rows, cb), lambda bi, n, j: (0, n))
    per_dir = pl.BlockSpec((2, 1, cb), lambda bi, n, j: (0, 0, n))
    gate_w = pl.BlockSpec((2, 1, cb, cb), lambda bi, n, j: (0, n, 0, 0))
    state = pl.BlockSpec((2, 1, 1, cb), lambda bi, n, j: (0, bi, 0, n))
    hf, hb, hfin = pl.pallas_call(
        _lru_scan_kernel,
        out_shape=(jax.ShapeDtypeStruct(x4.shape, BF16), jax.ShapeDtypeStruct(x4.shape, BF16),
                   jax.ShapeDtypeStruct(h0.shape, F32)),
        grid=(b, LRU_BLOCKS, nj),
        in_specs=[main(up), prev(up), nxt(up), main(down), prev(down), nxt(down),
                  chan(LRU_CONV_W), chan(1), gate_w, gate_w, per_dir, per_dir, per_dir, state],
        out_specs=(main(up), main(down), state),
        scratch_shapes=[pltpu.VMEM((2, 2, LRU_GATE_ROWS // wb + LRU_CONV_W - 1, wb, cb), F32)]
        + [pltpu.VMEM((2, nr, wb, cb), F32)] * 2 + [pltpu.VMEM((2, 1, cb), F32)],
        compiler_params=_cparams("parallel", "parallel", "arbitrary"),
        name="lru_scan",
    )(x4, x4, x4, x4, x4, x4, conv_w, conv_b.reshape(1, c), wa, wx,
      ba.reshape(2, 1, c), bx.reshape(2, 1, c), lam.reshape(2, 1, c), h0)
    return hf.reshape(b, t, c), hb.reshape(b, t, c), hfin


FFN_ROW_TILE = 512


def _mix_ffn_kernel(af_ref, ab_ref, gs_ref, xm_ref, afp_ref, abp_ref, gsp_ref, xp_ref, afn_ref, abn_ref,
                    gsn_ref, xn_ref, gain_ref, wo_ref, gate1_ref, g_ref, sh_ref, sc_ref, gate_ref, wup_ref,
                    cw_ref, cb_ref, wdn_ref, fg_ref, out_ref, y_ref, lat_ref, h_ref, act_ref,
                    *, head_norm, ctx, n_lat, final_norm):
    i = pl.program_id(1)
    tm, d = xm_ref.shape[1], xm_ref.shape[2]
    dff = wdn_ref.shape[0]
    lo = slice(BF16_ROWS - HALO, BF16_ROWS)
    hi = slice(0, HALO)

    def readout_in(af, ab, gs):
        o = af.astype(F32) + ab.astype(F32)
        if head_norm:
            dv = gain_ref.shape[-1]
            o = jnp.concatenate([_rms(o[:, c0:c0 + dv], gain_ref[...]) for c0 in range(0, d, dv)], axis=-1)
        return o * gs.astype(F32)

    g = g_ref[...]
    sh = _mod_row(sh_ref, ctx, n_lat)
    sc = _mod_row(sc_ref, ctx, n_lat)
    g_sc = g * (1.0 + sc)
    mod = lambda x: _rms(x, g_sc) + sh
    rows = tm + 2 * HALO
    y_ref[...] = jnp.concatenate(
        [readout_in(afp_ref[0, lo], abp_ref[0, lo], gsp_ref[0, lo]),
         readout_in(af_ref[0], ab_ref[0], gs_ref[0]),
         readout_in(afn_ref[0, hi], abn_ref[0, hi], gsn_ref[0, hi])], axis=0).astype(BF16)
    x_all = jnp.concatenate([xp_ref[0, lo], xm_ref[0], xn_ref[0, hi]], axis=0)
    lat_ref[...] = x_all + _mod_row(gate1_ref, ctx, n_lat) * jnp.dot(y_ref[...], wo_ref[...],
                                                                     preferred_element_type=F32)
    hp = jnp.where(i > 0, mod(lat_ref[0:HALO]), 0.0)
    hn = jnp.where(i < pl.num_programs(1) - 1, mod(lat_ref[HALO + tm:]), 0.0)
    h_ref[...] = jnp.concatenate([hp, mod(lat_ref[HALO:HALO + tm]), hn], axis=0).astype(BF16)
    nc = MXU_N
    for c0 in range(0, dff, nc):
        halves = []
        for base in (c0, dff + c0):
            u = jnp.dot(h_ref[...], wup_ref[:, base:base + nc], preferred_element_type=F32)
            cols = slice(base, base + nc)
            conv = (cb_ref[:, cols]
                    + cw_ref[0:1, cols] * pltpu.roll(u, 1, axis=0)[HALO:HALO + tm]
                    + cw_ref[1:2, cols] * u[HALO:HALO + tm]
                    + cw_ref[2:3, cols] * pltpu.roll(u, rows - 1, axis=0)[HALO:HALO + tm])
            halves.append(conv)
        act_ref[:, c0:c0 + nc] = (_silu(halves[0]) * halves[1]).astype(BF16)
    half = tm // 2
    for r0 in (0, half):
        z = jnp.dot(act_ref[r0:r0 + half], wdn_ref[...], preferred_element_type=F32)
        y = lat_ref[HALO + r0:HALO + r0 + half] + _mod_row(gate_ref, ctx, n_lat) * z
        out_ref[0, r0:r0 + half] = _rms(y, fg_ref[...]) if final_norm else y


def _mix_ffn(a_f, a_b, gs, x, mods, layer, head_gain, w_out, norm_g, w_up, conv_w, conv_b, w_down, final_g,
             ctx, n_lat, final_norm, casts=()):
    b, t, d = x.shape
    dff = w_down.shape[0]
    tm = _row_tile(t, FFN_ROW_TILE)
    nt = t // tm
    per = tm // BF16_ROWS
    tile = pl.BlockSpec((1, tm, d), lambda bi, i: (bi, i, 0))
    prev = pl.BlockSpec((1, BF16_ROWS, d), lambda bi, i: (bi, jnp.maximum(i * per - 1, 0), 0))
    nxt = pl.BlockSpec((1, BF16_ROWS, d), lambda bi, i: (bi, jnp.minimum((i + 1) * per, t // BF16_ROWS - 1), 0))
    head_norm = head_gain is not None
    gain = head_gain.reshape(1, -1) if head_norm else jnp.ones((1, LANES), F32)
    rows = tm + 2 * HALO
    (out,), cast_w = _side_cast_call(
        functools.partial(_mix_ffn_kernel, head_norm=head_norm, ctx=ctx, n_lat=n_lat, final_norm=final_norm),
        casts, b * nt, lambda bi, i: bi * nt + i,
        out_shape=(jax.ShapeDtypeStruct((b, t, d), F32),),
        grid=(b, nt),
        in_specs=[tile, tile, tile, tile, prev, prev, prev, prev, nxt, nxt, nxt, nxt,
                  _resident(gain.shape), _resident(w_out.shape), _mod_spec(layer, 2, d),
                  _resident((1, d)), _mod_spec(layer, 3, d), _mod_spec(layer, 4, d), _mod_spec(layer, 5, d),
                  _resident(w_up.shape), _resident(conv_w.shape), _resident((1, 2 * dff)),
                  _resident(w_down.shape), _resident((1, d))],
        out_specs=(tile,),
        scratch_shapes=[pltpu.VMEM((rows, d), BF16), pltpu.VMEM((rows, d), F32), pltpu.VMEM((rows, d), BF16),
                        pltpu.VMEM((tm, dff), BF16)],
        compiler_params=_cparams("parallel", "parallel"),
        name="mix_ffn",
        args=(a_f, a_b, gs, x, a_f, a_b, gs, x, a_f, a_b, gs, x, gain, w_out, mods,
              norm_g.reshape(1, d), mods, mods, mods, w_up, conv_w, conv_b.reshape(1, 2 * dff), w_down,
              final_g.reshape(1, d)))
    return out, cast_w


def kernel(x, c, ctx, c_ctx, w_ada, b_ada, norm_g, hg_w_in, hg_lb_logits, hg_gnorm, hg_w_out, lru_w_in,
           lru_conv_w, lru_conv_b, lru_wa, lru_ba, lru_wx, lru_bx, lru_lambda, lru_w_out, ffn_w_up,
           ffn_conv_w, ffn_conv_b, ffn_w_down, final_g):
    nb, _, d = x.shape
    depth = w_ada.shape[0]
    n_mixers = 2

    weights = {"hg_in": (hg_w_in, 1.0), "hg_out": (hg_w_out, 1.0), "lru_in": (lru_w_in, 1.0),
               "lru_out": (lru_w_out, 1.0), "lru_wa": (lru_wa, 0.5), "lru_wx": (lru_wx, 0.5),
               "ffn_up": (ffn_w_up, 1.0), "ffn_down": (ffn_w_down, 1.0)}
    ready = {}

    def bf(name, idx):
        if (name, idx) not in ready:
            ready[name, idx] = _layer_bf16(weights[name][0], idx, weights[name][1])
        return ready[name, idx]

    def hosted(call, *keys):
        keys = [k for k in keys if k not in ready]
        outs, cast_w = call(casts=[_SideCast(weights[n][0], i, weights[n][1]) for n, i in keys])
        ready.update(zip(keys, cast_w))
        return outs

    def mixer_keys(l):
        j = l // n_mixers
        names = ("hg_in", "hg_out") if l % n_mixers == 0 else ("lru_in", "lru_out", "lru_wa", "lru_wx")
        return [(n, j) for n in names]

    cond = jnp.concatenate([c, c_ctx[None, :], jnp.zeros((COND_ROWS - nb - 1, d), F32)], axis=0)
    mods = _ada(cond.T, w_ada, b_ada, nb + 1)

    lat, cx = x, ctx
    for l in range(depth):
        last = l == depth - 1
        j = l // n_mixers
        if l % n_mixers == 0:
            w_in, w_out = bf("hg_in", j), bf("hg_out", j)
            s0 = jnp.zeros((nb, 2, d // HG_PAIR, HG_PAIR, HG_PAIR), F32)
            lb_logits = jnp.swapaxes(hg_lb_logits, 0, 1)
            hg_in = functools.partial(_hg_in, mods=mods, layer=l, norm_g=norm_g[l, 0], w_in=w_in,
                                      lb_logits=lb_logits, layer_j=j, n_lat=nb)
            pc, _ = hg_in(cx, ctx=True)
            (oc_f, oc_b, s_ctx), _ = _hg_scan(pc[0], pc[1], pc[2], pc[3], s0)
            pl_ = hosted(functools.partial(hg_in, lat, ctx=False), ("ffn_down", l))
            ol_f, ol_b, _ = hosted(functools.partial(_hg_scan, pl_[0], pl_[1], pl_[2], pl_[3], s_ctx),
                                   ("ffn_up", l))
            mix_lat = (ol_f, ol_b, pl_[4])
            mix_ctx = (oc_f, oc_b, pc[4])
            head_gain = hg_gnorm[j]
        else:
            w_in, w_out = bf("lru_in", j), bf("lru_out", j)
            scan = functools.partial(_lru_scan, conv_w=lru_conv_w[j], conv_b=lru_conv_b[j],
                                     wa=bf("lru_wa", j), wx=bf("lru_wx", j),
                                     ba=lru_ba[j], bx=lru_bx[j], lam=lru_lambda[j])
            lru_in = functools.partial(_lru_in, mods=mods, layer=l, norm_g=norm_g[l, 0], w_in=w_in, n_lat=nb)
            wc = LRU_COL_BLOCK
            t_ctx = cx.shape[1]
            flip = lambda a: jnp.swapaxes(a.reshape(nb, wc, t_ctx // wc, d), 1, 2).reshape(nb, t_ctx, d)
            unflip = lambda a: jnp.swapaxes(a.reshape(nb, t_ctx // wc, wc, d), 1, 2).reshape(nb, t_ctx, d)
            (gg_c, xr_c), _ = lru_in(cx, ctx=True)
            hc_f, hc_b, h_ctx = scan(flip(xr_c), h0=jnp.zeros((2, nb, 1, d), F32), grid_w=wc)
            gg_l, xr_l = hosted(functools.partial(lru_in, lat, ctx=False), ("ffn_down", l))
            hl_f, hl_b, _ = scan(xr_l, h0=h_ctx, grid_w=GRID_W)
            mix_lat = (hl_f, hl_b, gg_l)
            mix_ctx = (hc_f, hc_b, gg_c)
            head_gain = None
        mix_ffn = functools.partial(_mix_ffn, mods=mods, layer=l, head_gain=head_gain, w_out=w_out,
                                    norm_g=norm_g[l, 1], w_up=bf("ffn_up", l), conv_w=ffn_conv_w[l],
                                    conv_b=ffn_conv_b[l], w_down=bf("ffn_down", l), final_g=final_g, n_lat=nb)
        nxt = [] if last else mixer_keys(l + 1) + ([("ffn_up", l + 1)] if (l + 1) % n_mixers else [])
        lat = hosted(functools.partial(mix_ffn, *mix_lat, lat, ctx=False, final_norm=last), *nxt)
        if not last:
            if head_gain is None:
                mix_ctx = (unflip(mix_ctx[0]), unflip(mix_ctx[1]), mix_ctx[2])
            cx, _ = mix_ffn(*mix_ctx, cx, ctx=True, final_norm=False)
    return lat
```

```python
import functools

import jax
import jax.numpy as jnp
from jax import lax
from jax.experimental import pallas as pl
from jax.experimental.pallas import tpu as pltpu

F32 = jnp.float32
BF16 = jnp.bfloat16

EPS = 1e-6
GRID_W = 64
HG_DK = 128
CHUNK = 64
LRU_BLOCKS = 4
LRU_C = 8.0
LRU_CONV_W = 4
FFN_CONV_W = 3

SUBLANES = 8
LANES = 128
BF16_ROWS = 16
MXU_N = 256
VMEM_LIMIT_BYTES = 56 * 1024 * 1024
COND_ROWS = SUBLANES


def _cparams(*sem):
    return pltpu.CompilerParams(dimension_semantics=sem, vmem_limit_bytes=VMEM_LIMIT_BYTES)


def _resident(shape):
    zeros = (0,) * len(shape)
    return pl.BlockSpec(shape, lambda *_: zeros, pipeline_mode=pl.Buffered(1))


IN_PROJ_ROW_TILE = 1024


def _row_tile(t, largest):
    tm = largest
    while t % tm:
        tm //= 2
    return tm


def _sigmoid(x):
    return 0.5 + 0.5 * jnp.tanh(0.5 * x)


def _silu(x):
    return x * _sigmoid(x)


def _gelu_tanh(x):
    return 0.5 * x * (1.0 + jnp.tanh(0.7978845608028654 * (x + 0.044715 * (x * x * x))))


def _rms(x, g):
    return x * lax.rsqrt(jnp.mean(x * x, axis=-1, keepdims=True) + EPS) * g


def _mod_row(mod_ref, ctx, n_lat):
    if ctx:
        return mod_ref[0, n_lat:n_lat + 1, :]
    return mod_ref[0, pl.ds(pl.program_id(0), 1), :]


def _mod_spec(k, d):
    return pl.BlockSpec((1, COND_ROWS, d), lambda *_: (0, 0, k))


CAST_BLOCK_BYTES = 4 * 1024 * 1024


def _cast_kernel(w_ref, o_ref, *, scale):
    w = w_ref[0]
    o_ref[...] = (w if scale == 1.0 else w * scale).astype(o_ref.dtype)


def _layer_bf16(w, layer, scale=1.0):
    shape = w.shape[1:]
    cols = shape[-1]
    rows = w[0].size // cols
    tr = rows
    while tr * cols * 4 > CAST_BLOCK_BYTES and tr % (2 * BF16_ROWS) == 0:
        tr //= 2
    out = pl.pallas_call(
        functools.partial(_cast_kernel, scale=scale),
        out_shape=jax.ShapeDtypeStruct((rows, cols), BF16),
        grid=(rows // tr,),
        in_specs=[pl.BlockSpec((1, tr, cols), lambda i: (layer, i, 0))],
        out_specs=pl.BlockSpec((tr, cols), lambda i: (i, 0)),
        compiler_params=_cparams("parallel"),
        name="cast_bf16",
    )(w.reshape(w.shape[0], rows, cols))
    return out.reshape(shape)


class _SideCast:
    def __init__(self, w, layer, scale=1.0):
        self.shape = w.shape[1:]
        self.cols = w.shape[-1]
        self.rows = w[0].size // self.cols
        self.w = w.reshape(w.shape[0], self.rows, self.cols)
        self.layer = layer
        self.scale = scale

    def specs(self, n_steps, step):
        slab, rem = divmod(self.rows, n_steps)
        assert rem == 0 and slab % BF16_ROWS == 0, (self.rows, n_steps)
        layer = self.layer
        return ([pl.BlockSpec((1, slab, self.cols), lambda *ids: (layer, step(*ids), 0))],
                pl.BlockSpec((slab, self.cols), lambda *ids: (step(*ids), 0)),
                jax.ShapeDtypeStruct((self.rows, self.cols), BF16))

    def operands(self):
        return [self.w]

    def emit(self, in_refs, o_ref):
        w = in_refs[0][0]
        o_ref[...] = (w if self.scale == 1.0 else w * self.scale).astype(o_ref.dtype)

    def finish(self, out):
        return out.reshape(self.shape)


def _with_side_jobs(body, n_in, n_out, jobs, job_n_in):
    n_ji = sum(job_n_in)
    n_j = len(jobs)

    def kernel(*refs):
        ins, j_in = refs[:n_in], refs[n_in:n_in + n_ji]
        outs = refs[n_in + n_ji:n_in + n_ji + n_out]
        j_out = refs[n_in + n_ji + n_out:n_in + n_ji + n_out + n_j]
        at = 0
        for job, k, o_ref in zip(jobs, job_n_in, j_out):
            job.emit(j_in[at:at + k], o_ref)
            at += k
        body(*ins, *outs, *refs[n_in + n_ji + n_out + n_j:])

    return kernel


def _side_cast_call(body, jobs, n_steps, step, *, out_shape, in_specs, out_specs, args, **kw):
    spec3 = [job.specs(n_steps, step) for job in jobs]
    outs = pl.pallas_call(
        _with_side_jobs(body, len(in_specs), len(out_shape), jobs, [len(s[0]) for s in spec3]),
        out_shape=tuple(out_shape) + tuple(s[2] for s in spec3),
        in_specs=list(in_specs) + [spec for s in spec3 for spec in s[0]],
        out_specs=tuple(out_specs) + tuple(s[1] for s in spec3),
        **kw,
    )(*args, *[a for job in jobs for a in job.operands()])
    n = len(out_shape)
    return outs[:n], [job.finish(o) for o, job in zip(outs[n:], jobs)]


def _ada_kernel(ct_ref, w_ref, b_ref, o_ref, *, n_cond):
    ct = ct_ref[...]
    s = _silu(ct)
    w = w_ref[0]
    o_ref[0] = jnp.zeros(o_ref.shape[1:], F32) + b_ref[0]
    for m in range(n_cond):
        o_ref[0, m:m + 1, :] = jnp.sum(w * s[:, m:m + 1], axis=0, keepdims=True) + b_ref[0]


def _ada(cond_t, w_ada, b_ada, n_cond, layer):
    depth, d, n = w_ada.shape
    tn = 1024
    return pl.pallas_call(
        functools.partial(_ada_kernel, n_cond=n_cond),
        out_shape=jax.ShapeDtypeStruct((1, COND_ROWS, n), F32),
        grid=(n // tn,),
        in_specs=[
            pl.BlockSpec((d, COND_ROWS), lambda j: (0, 0)),
            pl.BlockSpec((1, d, tn), lambda j: (layer, 0, j)),
            pl.BlockSpec((1, 1, tn), lambda j: (layer, 0, j)),
        ],
        out_specs=pl.BlockSpec((1, COND_ROWS, tn), lambda j: (0, 0, j)),
        compiler_params=_cparams("parallel"),
        name="ada",
    )(cond_t, w_ada, b_ada.reshape(depth, 1, n))


def _split2(x):
    hi = x.astype(BF16)
    return hi, (x - hi.astype(F32)).astype(BF16)


def _hg_in_kernel(x_ref, g_ref, sh_ref, sc_ref, w_ref, lbl_ref, q_ref, lff_ref, lfb_ref, v_ref, gs_ref,
                  *, ctx, n_lat, layer_j):
    d = x_ref.shape[-1]
    sh = _mod_row(sh_ref, ctx, n_lat)
    sc = _mod_row(sc_ref, ctx, n_lat)
    h = (_rms(x_ref[0], g_ref[...] * (1.0 + sc)) + sh).astype(BF16)
    nc = 2 * MXU_N
    for c0 in range(0, 5 * d, nc):
        part, p0 = divmod(c0, d)
        z = jnp.dot(h, w_ref[:, c0:c0 + nc], preferred_element_type=F32)
        cols = slice(p0, p0 + nc)
        if part == 0:
            q_ref[0, :, cols] = _silu(z).astype(BF16)
        elif part in (1, 2):
            lg = lbl_ref[part - 1, :, cols]
            e = jnp.exp(lg - jnp.max(lg, axis=0, keepdims=True))
            lb = jnp.sum(e[:layer_j + 1], axis=0, keepdims=True) / jnp.sum(e, axis=0, keepdims=True)
            f = lb + (1.0 - lb) * _sigmoid(z)
            (lff_ref if part == 1 else lfb_ref)[0, :, cols] = jnp.log(f)
        elif part == 3:
            v_ref[0, :, cols] = z.astype(BF16)
        else:
            gs_ref[0, :, cols] = _silu(z).astype(BF16)


def _hg_in(x, mods, norm_g, w_in, lb_logits, layer_j, ctx, n_lat, casts=()):
    b, t, d = x.shape
    tm = _row_tile(t, IN_PROJ_ROW_TILE)
    nt = t // tm
    tile = pl.BlockSpec((1, tm, d), lambda bi, i: (bi, i, 0))
    sds = lambda dt: jax.ShapeDtypeStruct((b, t, d), dt)
    return _side_cast_call(
        functools.partial(_hg_in_kernel, ctx=ctx, n_lat=n_lat, layer_j=layer_j),
        casts, b * nt, lambda bi, i: bi * nt + i,
        out_shape=(sds(BF16), sds(F32), sds(F32), sds(BF16), sds(BF16)),
        grid=(b, nt),
        in_specs=[tile, _resident((1, d)), _mod_spec(0, d), _mod_spec(1, d),
                  _resident(w_in.shape), _resident(lb_logits.shape)],
        out_specs=(tile, tile, tile, tile, tile),
        compiler_params=_cparams("parallel", "parallel"),
        name="hg_in",
        args=(x, norm_g.reshape(1, d), mods, mods, w_in, lb_logits))


HG_PAIR = 2 * HG_DK
HG_SCAN_BLOCK = 512


def _hg_scan_kernel(qf_ref, lf_ref, vf_ref, qb_ref, lb_ref, vb_ref, s0_ref, of_ref, ob_ref, sfin_ref,
                    st_ref, st16_ref, qe_ref, qs_ref, kebd_ref, kdbd_ref, vbd_ref, dec_ref):
    i = pl.program_id(1)
    tb, d = qf_ref.shape[1], qf_ref.shape[2]
    n_chunks = tb // CHUNK
    n_pairs = d // HG_PAIR

    @pl.when(i == 0)
    def _():
        st_ref[...] = s0_ref[0]
        for dn in range(2):
            for p in range(n_pairs):
                st16_ref[dn, p] = s0_ref[0, dn, p].T.astype(BF16)
        kebd_ref[...] = jnp.zeros(kebd_ref.shape, BF16)
        kdbd_ref[...] = jnp.zeros(kdbd_ref.shape, BF16)
        vbd_ref[...] = jnp.zeros(vbd_ref.shape, BF16)

    grp = min(tb, MXU_N)
    row = lax.broadcasted_iota(jnp.int32, (grp, grp), 0)
    col = lax.broadcasted_iota(jnp.int32, (grp, grp), 1)
    same_chunk = (row // CHUNK) == (col // CHUNK)
    r_in = lax.broadcasted_iota(jnp.int32, (CHUNK, 2 * CHUNK), 0)
    c_in = lax.broadcasted_iota(jnp.int32, (CHUNK, 2 * CHUNK), 1) % CHUNK
    dirs = (
        (qf_ref, lf_ref, vf_ref, of_ref, same_chunk & (row >= col), r_in >= c_in, CHUNK // 2, CHUNK - 1),
        (qb_ref, lb_ref, vb_ref, ob_ref, same_chunk & (row <= col), r_in <= c_in, CHUNK // 2 - 1, 0),
    )

    for dn, (q_ref, lfd_ref, v_ref, _, keep_blk, _, ref_row, last_row) in enumerate(dirs):
        tri = jnp.where(keep_blk, 1.0, 0.0).astype(BF16)
        lf = lfd_ref[0]
        b = jnp.concatenate(
            [sum(jnp.dot(tri, part, preferred_element_type=F32) for part in _split2(lf[r0:r0 + grp]))
             for r0 in range(0, tb, grp)], axis=0)
        for c in range(n_chunks):
            rows = slice(c * CHUNK, (c + 1) * CHUNK)
            bc = b[rows]
            b_mid = bc[ref_row:ref_row + 1, :]
            b_end = bc[last_row:last_row + 1, :]
            e_q = jnp.exp(bc - b_mid)
            qe = q_ref[0, rows, :].astype(F32) * e_q
            ke = (1.0 - jnp.exp(lf[rows])) * (1.0 / e_q)
            qe_ref[dn, rows, :] = qe.astype(BF16)
            qs_ref[dn, rows, :] = (qe * jnp.exp(b_mid)).astype(BF16)
            ke16 = ke.astype(BF16)
            kd16 = (ke * jnp.exp(b_end - b_mid)).astype(BF16)
            for h in range(2 * n_pairs):
                p, hh = divmod(h, 2)
                cols = slice(h * HG_DK, (h + 1) * HG_DK)
                blk = (slice(hh * CHUNK, (hh + 1) * CHUNK), slice(hh * HG_DK, (hh + 1) * HG_DK))
                kebd_ref[dn, c, p, blk[0], blk[1]] = ke16[:, cols]
                kdbd_ref[dn, c, p, blk[0], blk[1]] = kd16[:, cols]
                vbd_ref[dn, c, p, blk[0], blk[1]] = v_ref[0, rows, cols]
            dec_ref[dn, c:c + 1, :] = jnp.exp(b_end)

    for c in range(n_chunks):
        for dn, (_, _, v_ref, o_ref, _, keep, _, _) in enumerate(dirs):
            cc = c if dn == 0 else n_chunks - 1 - c
            rows = slice(cc * CHUNK, (cc + 1) * CHUNK)
            for p in range(n_pairs):
                pc = slice(p * HG_PAIR, (p + 1) * HG_PAIR)
                scores = lax.dot_general(qe_ref[dn, rows, pc], kebd_ref[dn, cc, p], (((1,), (1,)), ((), ())),
                                         preferred_element_type=F32)
                scores = jnp.where(keep, scores, 0.0).astype(BF16)
                o = jnp.dot(scores, vbd_ref[dn, cc, p], preferred_element_type=F32)
                o = o + jnp.dot(qs_ref[dn, rows, pc], st16_ref[dn, p], preferred_element_type=F32)
                o_ref[0, rows, pc] = o.astype(BF16)
                v_stack = jnp.concatenate(
                    [v_ref[0, rows, p * HG_PAIR:p * HG_PAIR + HG_DK],
                     v_ref[0, rows, p * HG_PAIR + HG_DK:(p + 1) * HG_PAIR]], axis=0)
                kv = lax.dot_general(v_stack, kdbd_ref[dn, cc, p], (((0,), (0,)), ((), ())),
                                     preferred_element_type=F32)
                for hh in range(2):
                    sl = slice(hh * HG_DK, (hh + 1) * HG_DK)
                    cols = slice(p * HG_PAIR + hh * HG_DK, p * HG_PAIR + (hh + 1) * HG_DK)
                    new = st_ref[dn, p, sl, sl] * dec_ref[dn, cc:cc + 1, cols] + kv[:, sl]
                    st_ref[dn, p, sl, sl] = new
                    st16_ref[dn, p, sl, sl] = new.T.astype(BF16)

    @pl.when(i == pl.num_programs(1) - 1)
    def _():
        sfin_ref[0] = st_ref[...]


def _hg_scan(q, lf_f, lf_b, v, s0, casts=()):
    b, t, d = q.shape
    tb = _row_tile(t, HG_SCAN_BLOCK)
    nblk = t // tb
    fwd = pl.BlockSpec((1, tb, d), lambda bi, i: (bi, i, 0))
    bwd = pl.BlockSpec((1, tb, d), lambda bi, i: (bi, nblk - 1 - i, 0))
    st_spec = pl.BlockSpec((1,) + s0.shape[1:], lambda bi, i: (bi, 0, 0, 0, 0))
    blockdiag = pltpu.VMEM((2, tb // CHUNK, d // HG_PAIR, 2 * CHUNK, HG_PAIR), BF16)
    return _side_cast_call(
        _hg_scan_kernel, casts, b * nblk, lambda bi, i: bi * nblk + i,
        out_shape=(jax.ShapeDtypeStruct((b, t, d), BF16), jax.ShapeDtypeStruct((b, t, d), BF16),
                   jax.ShapeDtypeStruct(s0.shape, F32)),
        grid=(b, nblk),
        in_specs=[fwd, fwd, fwd, bwd, bwd, bwd, st_spec],
        out_specs=(fwd, bwd, st_spec),
        scratch_shapes=[pltpu.VMEM(s0.shape[1:], F32), pltpu.VMEM(s0.shape[1:], BF16),
                        pltpu.VMEM((2, tb, d), BF16), pltpu.VMEM((2, tb, d), BF16),
                        blockdiag, blockdiag, blockdiag, pltpu.VMEM((2, tb // CHUNK, d), F32)],
        compiler_params=_cparams("parallel", "arbitrary"),
        name="hg_scan",
        args=(q, lf_f, v, q, lf_b, v, s0))


def _lru_in_kernel(x_ref, g_ref, sh_ref, sc_ref, w_ref, gg_ref, xr_ref, *, ctx, n_lat):
    d = x_ref.shape[-1]
    sh = _mod_row(sh_ref, ctx, n_lat)
    sc = _mod_row(sc_ref, ctx, n_lat)
    h = (_rms(x_ref[0], g_ref[...] * (1.0 + sc)) + sh).astype(BF16)
    nc = 2 * MXU_N
    for c0 in range(0, 2 * d, nc):
        part, p0 = divmod(c0, d)
        z = jnp.dot(h, w_ref[:, c0:c0 + nc], preferred_element_type=F32)
        if part == 0:
            gg_ref[0, :, p0:p0 + nc] = _gelu_tanh(z).astype(BF16)
        else:
            xr_ref[0, :, p0:p0 + nc] = z


def _lru_in(x, mods, norm_g, w_in, ctx, n_lat, casts=()):
    b, t, d = x.shape
    tm = _row_tile(t, IN_PROJ_ROW_TILE)
    nt = t // tm
    tile = pl.BlockSpec((1, tm, d), lambda bi, i: (bi, i, 0))
    return _side_cast_call(
        functools.partial(_lru_in_kernel, ctx=ctx, n_lat=n_lat),
        casts, b * nt, lambda bi, i: bi * nt + i,
        out_shape=(jax.ShapeDtypeStruct((b, t, d), BF16), jax.ShapeDtypeStruct((b, t, d), F32)),
        grid=(b, nt),
        in_specs=[tile, _resident((1, d)), _mod_spec(0, d), _mod_spec(1, d),
                  _resident(w_in.shape)],
        out_specs=(tile, tile),
        compiler_params=_cparams("parallel", "parallel"),
        name="lru_in",
        args=(x, norm_g.reshape(1, d), mods, mods, w_in))


LRU_COL_BLOCK = BF16_ROWS
LRU_GATE_ROWS = MXU_N
LRU_SCAN_SEGS = 2
LRU_SCAN_UNROLL = 8
LOG2_E = 1.4426950408889634
LN_2 = 0.6931471805599453
HALO = SUBLANES


def _softplus(x):
    y = jnp.exp(-jnp.abs(x))
    u = 1.0 + y
    log1p = jnp.where(u == 1.0, y, jnp.log(u) * (y / (u - 1.0)))
    return jnp.maximum(x, 0.0) + log1p


def _lru_scan_kernel(xf_ref, xfp_ref, xfn_ref, xb_ref, xbp_ref, xbn_ref, cw_ref, cb_ref, wa_ref, wx_ref,
                     ba_ref, bx_ref, lam_ref, h0_ref, hf_ref, hb_ref, hfin_ref, xpad_ref, a_ref, u_ref,
                     carry_ref):
    j = pl.program_id(2)
    nj = pl.num_programs(2)
    _, nr, wb, cb = xf_ref.shape

    @pl.when(j == 0)
    def _():
        carry_ref[...] = h0_ref[:, 0]

    col = lax.broadcasted_iota(jnp.int32, (wb, cb), 0)
    left = LRU_CONV_W - 1 - (LRU_CONV_W - 1) // 2
    right = LRU_CONV_W - 1 - left
    rc = LRU_GATE_ROWS // wb
    n_ch = nr // rc
    x_refs = (xf_ref, xb_ref)
    for dn, (x_ref, xp_ref, xn_ref, jb) in enumerate(((xf_ref, xfp_ref, xfn_ref, j),
                                                       (xb_ref, xbp_ref, xbn_ref, nj - 1 - j))):
        last_end = min(n_ch - 1, 1)
        for k in range(left):
            edge = jnp.where(jb > 0, xp_ref[0, HALO - left + k, wb - 1:wb, :], 0.0)
            xpad_ref[dn, 0, k] = jnp.where(col == 0, edge, pltpu.roll(x_ref[0, nr - left + k], 1, axis=0))
        for k in range(right):
            edge = jnp.where(jb < nj - 1, xn_ref[0, k, 0:1, :], 0.0)
            xpad_ref[dn, last_end, left + rc + k] = jnp.where(col == wb - 1, edge,
                                                              pltpu.roll(x_ref[0, k], wb - 1, axis=0))
        if n_ch == 1:
            xpad_ref[dn, 0, left:left + rc] = x_ref[0]
        else:
            xpad_ref[dn, 0, left:] = x_ref[0, 0:rc + right]
            xpad_ref[dn, 1, 0:left + rc] = x_ref[0, nr - rc - left:nr]

    cw = [cw_ref[k:k + 1, :] for k in range(LRU_CONV_W)]
    cbias = cb_ref[...]
    k2 = [(-0.5 * LRU_C * LOG2_E) * _softplus(-lam_ref[dn]) for dn in range(2)]
    half_ba = [0.5 * ba_ref[dn] for dn in range(2)]
    half_bx = [0.5 * bx_ref[dn] for dn in range(2)]

    def gates(dn, rows, tap):
        xc = cbias + sum(cw[k] * tap(k) for k in range(LRU_CONV_W))
        xc = xc.reshape(rc * wb, cb)
        xcb = xc.astype(BF16)
        tr = jnp.tanh(jnp.dot(xcb, wa_ref[dn, 0], preferred_element_type=F32) + half_ba[dn])
        ti = jnp.tanh(jnp.dot(xcb, wx_ref[dn, 0], preferred_element_type=F32) + half_bx[dn])
        log2_a = k2[dn] * tr + k2[dn]
        a = jnp.exp2(log2_a)
        q2 = (a * a * 0.25 + 0.25) * jnp.tanh(log2_a * (-LN_2))
        half_mult = jnp.where(q2 > 0.0, q2 * lax.rsqrt(q2), 0.0)
        a_ref[dn, rows] = a.reshape(rc, wb, cb)
        u_ref[dn, rows] = (half_mult * (xc * ti + xc)).reshape(rc, wb, cb)

    for end, r0 in ((0, 0), (1, nr - rc))[:min(n_ch, 2)]:
        for dn in range(2):
            gates(dn, slice(r0, r0 + rc), lambda k, dn=dn, end=end: xpad_ref[dn, end, k:k + rc])

    def interior_chunk(ci, carry):
        r0 = pl.multiple_of(ci * rc, rc)
        for dn in range(2):
            gates(dn, pl.ds(r0, rc), lambda k, dn=dn: x_refs[dn][0, pl.ds(r0 - left + k, rc)])
        return carry

    if n_ch > 2:
        lax.fori_loop(1, n_ch - 1, interior_chunk, 0)

    n_seg = LRU_SCAN_SEGS
    seg = nr // n_seg
    unroll = min(seg, LRU_SCAN_UNROLL)
    row_of = lambda dn, s, rr: s * seg + (rr if dn == 0 else seg - 1 - rr)
    units = [(dn, s) for dn in range(2) for s in range(n_seg)]

    def sweep1(rr, carry):
        out = []
        for (dn, s), (h, p) in zip(units, carry):
            a = a_ref[dn, row_of(dn, s, rr)]
            out.append((a * h + u_ref[dn, row_of(dn, s, rr)], a * p))
        return tuple(out)

    zero = jnp.zeros((wb, cb), F32)
    one = jnp.ones((wb, cb), F32)
    ends = lax.fori_loop(0, seg, sweep1, ((zero, one),) * len(units), unroll=unroll)

    entry = {}
    for dn in range(2):
        c_in = carry_ref[dn]
        wls = range(wb) if dn == 0 else range(wb - 1, -1, -1)
        segs = range(n_seg) if dn == 0 else range(n_seg - 1, -1, -1)
        for s in segs:
            entry[dn, s] = zero
        for wl in wls:
            for s in segs:
                h_end, p_end = ends[units.index((dn, s))]
                entry[dn, s] = jnp.where(col == wl, c_in, entry[dn, s])
                c_in = p_end[wl:wl + 1, :] * c_in + h_end[wl:wl + 1, :]
        carry_ref[dn] = c_in

    o_refs = (hf_ref, hb_ref)

    def sweep2(rr, hs):
        out = []
        for (dn, s), h in zip(units, hs):
            r = row_of(dn, s, rr)
            h = a_ref[dn, r] * h + u_ref[dn, r]
            o_refs[dn][0, r] = h.astype(o_refs[dn].dtype)
            out.append(h)
        return tuple(out)

    lax.fori_loop(0, seg, sweep2, tuple(entry[u] for u in units), unroll=unroll)

    @pl.when(j == nj - 1)
    def _():
        hfin_ref[:, 0] = carry_ref[...]


def _lru_scan(xr, conv_w, conv_b, wa, wx, ba, bx, lam, h0, grid_w):
    b, t, c = xr.shape
    nr = t // grid_w
    cb = c // LRU_BLOCKS
    wb = LRU_COL_BLOCK
    nj = grid_w // wb
    x4 = xr.reshape(b, nr, grid_w, c)
    up = lambda j: j
    down = lambda j: nj - 1 - j
    main = lambda cj: pl.BlockSpec((1, nr, wb, cb), lambda bi, n, j: (bi, 0, cj(j), n))
    prev = lambda cj: pl.BlockSpec((1, HALO, wb, cb),
                                   lambda bi, n, j: (bi, nr // HALO - 1, jnp.maximum(cj(j) - 1, 0), n))
    nxt = lambda cj: pl.BlockSpec((1, HALO, wb, cb),
                                  lambda bi, n, j: (bi, 0, jnp.minimum(cj(j) + 1, nj - 1), n))
    chan = lambda rows: pl.BlockSpec((rows, cb), lambda bi, n, j: (0, n))
    per_dir = pl.BlockSpec((2, 1, cb), lambda bi, n, j: (0, 0, n))
    gate_w = pl.BlockSpec((2, 1, cb, cb), lambda bi, n, j: (0, n, 0, 0))
    state = pl.BlockSpec((2, 1, 1, cb), lambda bi, n, j: (0, bi, 0, n))
    hf, hb, hfin = pl.pallas_call(
        _lru_scan_kernel,
        out_shape=(jax.ShapeDtypeStruct(x4.shape, BF16), jax.ShapeDtypeStruct(x4.shape, BF16),
                   jax.ShapeDtypeStruct(h0.shape, F32)),
        grid=(b, LRU_BLOCKS, nj),
        in_specs=[main(up), prev(up), nxt(up), main(down), prev(down), nxt(down),
                  chan(LRU_CONV_W), chan(1), gate_w, gate_w, per_dir, per_dir, per_dir, state],
        out_specs=(main(up), main(down), state),
        scratch_shapes=[pltpu.VMEM((2, 2, LRU_GATE_ROWS // wb + LRU_CONV_W - 1, wb, cb), F32)]
        + [pltpu.VMEM((2, nr, wb, cb), F32)] * 2 + [pltpu.VMEM((2, 1, cb), F32)],
        compiler_params=_cparams("parallel", "parallel", "arbitrary"),
        name="lru_scan",
    )(x4, x4, x4, x4, x4, x4, conv_w, conv_b.reshape(1, c), wa, wx,
      ba.reshape(2, 1, c), bx.reshape(2, 1, c), lam.reshape(2, 1, c), h0)
    return hf.reshape(b, t, c), hb.reshape(b, t, c), hfin


FFN_ROW_TILE = 512


def _mix_ffn_kernel(af_ref, ab_ref, gs_ref, xm_ref, afp_ref, abp_ref, gsp_ref, xp_ref, afn_ref, abn_ref,
                    gsn_ref, xn_ref, gain_ref, wo_ref, gate1_ref, g_ref, sh_ref, sc_ref, gate_ref, wup_ref,
                    cw_ref, cb_ref, wdn_ref, fg_ref, out_ref, y_ref, lat_ref, h_ref, act_ref,
                    *, head_norm, ctx, n_lat, final_norm):
    i = pl.program_id(1)
    tm, d = xm_ref.shape[1], xm_ref.shape[2]
    dff = wdn_ref.shape[0]
    lo = slice(BF16_ROWS - HALO, BF16_ROWS)
    hi = slice(0, HALO)

    def readout_in(af, ab, gs):
        o = af.astype(F32) + ab.astype(F32)
        if head_norm:
            dv = gain_ref.shape[-1]
            o = jnp.concatenate([_rms(o[:, c0:c0 + dv], gain_ref[...]) for c0 in range(0, d, dv)], axis=-1)
        return o * gs.astype(F32)

    g = g_ref[...]
    sh = _mod_row(sh_ref, ctx, n_lat)
    sc = _mod_row(sc_ref, ctx, n_lat)
    g_sc = g * (1.0 + sc)
    mod = lambda x: _rms(x, g_sc) + sh
    rows = tm + 2 * HALO
    y_ref[...] = jnp.concatenate(
        [readout_in(afp_ref[0, lo], abp_ref[0, lo], gsp_ref[0, lo]),
         readout_in(af_ref[0], ab_ref[0], gs_ref[0]),
         readout_in(afn_ref[0, hi], abn_ref[0, hi], gsn_ref[0, hi])], axis=0).astype(BF16)
    x_all = jnp.concatenate([xp_ref[0, lo], xm_ref[0], xn_ref[0, hi]], axis=0)
    lat_ref[...] = x_all + _mod_row(gate1_ref, ctx, n_lat) * jnp.dot(y_ref[...], wo_ref[...],
                                                                     preferred_element_type=F32)
    hp = jnp.where(i > 0, mod(lat_ref[0:HALO]), 0.0)
    hn = jnp.where(i < pl.num_programs(1) - 1, mod(lat_ref[HALO + tm:]), 0.0)
    h_ref[...] = jnp.concatenate([hp, mod(lat_ref[HALO:HALO + tm]), hn], axis=0).astype(BF16)
    nc = MXU_N
    for c0 in range(0, dff, nc):
        halves = []
        for base in (c0, dff + c0):
            u = jnp.dot(h_ref[...], wup_ref[:, base:base + nc], preferred_element_type=F32)
            cols = slice(base, base + nc)
            conv = (cb_ref[:, cols]
                    + cw_ref[0:1, cols] * pltpu.roll(u, 1, axis=0)[HALO:HALO + tm]
                    + cw_ref[1:2, cols] * u[HALO:HALO + tm]
                    + cw_ref[2:3, cols] * pltpu.roll(u, rows - 1, axis=0)[HALO:HALO + tm])
            halves.append(conv)
        act_ref[:, c0:c0 + nc] = (_silu(halves[0]) * halves[1]).astype(BF16)
    half = tm // 2
    for r0 in (0, half):
        z = jnp.dot(act_ref[r0:r0 + half], wdn_ref[...], preferred_element_type=F32)
        y = lat_ref[HALO + r0:HALO + r0 + half] + _mod_row(gate_ref, ctx, n_lat) * z
        out_ref[0, r0:r0 + half] = _rms(y, fg_ref[...]) if final_norm else y


def _mix_ffn(a_f, a_b, gs, x, mods, head_gain, w_out, norm_g, w_up, conv_w, conv_b, w_down, final_g,
             ctx, n_lat, final_norm, casts=()):
    b, t, d = x.shape
    dff = w_down.shape[0]
    assert conv_w.shape[0] == FFN_CONV_W
    tm = _row_tile(t, FFN_ROW_TILE)
    nt = t // tm
    per = tm // BF16_ROWS
    tile = pl.BlockSpec((1, tm, d), lambda bi, i: (bi, i, 0))
    prev = pl.BlockSpec((1, BF16_ROWS, d), lambda bi, i: (bi, jnp.maximum(i * per - 1, 0), 0))
    nxt = pl.BlockSpec((1, BF16_ROWS, d), lambda bi, i: (bi, jnp.minimum((i + 1) * per, t // BF16_ROWS - 1), 0))
    head_norm = head_gain is not None
    gain = head_gain.reshape(1, -1) if head_norm else jnp.ones((1, LANES), F32)
    rows = tm + 2 * HALO
    (out,), cast_w = _side_cast_call(
        functools.partial(_mix_ffn_kernel, head_norm=head_norm, ctx=ctx, n_lat=n_lat, final_norm=final_norm),
        casts, b * nt, lambda bi, i: bi * nt + i,
        out_shape=(jax.ShapeDtypeStruct((b, t, d), F32),),
        grid=(b, nt),
        in_specs=[tile, tile, tile, tile, prev, prev, prev, prev, nxt, nxt, nxt, nxt,
                  _resident(gain.shape), _resident(w_out.shape), _mod_spec(2, d),
                  _resident((1, d)), _mod_spec(3, d), _mod_spec(4, d), _mod_spec(5, d),
                  _resident(w_up.shape), _resident(conv_w.shape), _resident((1, 2 * dff)),
                  _resident(w_down.shape), _resident((1, d))],
        out_specs=(tile,),
        scratch_shapes=[pltpu.VMEM((rows, d), BF16), pltpu.VMEM((rows, d), F32), pltpu.VMEM((rows, d), BF16),
                        pltpu.VMEM((tm, dff), BF16)],
        compiler_params=_cparams("parallel", "parallel"),
        name="mix_ffn",
        args=(a_f, a_b, gs, x, a_f, a_b, gs, x, a_f, a_b, gs, x, gain, w_out, mods,
              norm_g.reshape(1, d), mods, mods, mods, w_up, conv_w, conv_b.reshape(1, 2 * dff), w_down,
              final_g.reshape(1, d)))
    return out, cast_w


def kernel(x, c, ctx, c_ctx, w_ada, b_ada, norm_g, hg_w_in, hg_lb_logits, hg_gnorm, hg_w_out, lru_w_in,
           lru_conv_w, lru_conv_b, lru_wa, lru_ba, lru_wx, lru_bx, lru_lambda, lru_w_out, ffn_w_up,
           ffn_conv_w, ffn_conv_b, ffn_w_down, final_g):
    nb, _, d = x.shape
    depth = w_ada.shape[0]
    n_mixers = 2

    weights = {"hg_in": (hg_w_in, 1.0), "hg_out": (hg_w_out, 1.0), "lru_in": (lru_w_in, 1.0),
               "lru_out": (lru_w_out, 1.0), "lru_wa": (lru_wa, 0.5), "lru_wx": (lru_wx, 0.5),
               "ffn_up": (ffn_w_up, 1.0), "ffn_down": (ffn_w_down, 1.0)}
    ready = {}

    def bf(name, idx):
        if (name, idx) not in ready:
            ready[name, idx] = _layer_bf16(weights[name][0], idx, weights[name][1])
        return ready[name, idx]

    def hosted(call, *keys):
        keys = [k for k in keys if k not in ready]
        outs, cast_w = call(casts=[_SideCast(weights[n][0], i, weights[n][1]) for n, i in keys])
        ready.update(zip(keys, cast_w))
        return outs

    def mixer_keys(l):
        j = l // n_mixers
        names = ("hg_in", "hg_out") if l % n_mixers == 0 else ("lru_in", "lru_out", "lru_wa", "lru_wx")
        return [(n, j) for n in names]

    cond = jnp.concatenate([c, c_ctx[None, :], jnp.zeros((COND_ROWS - nb - 1, d), F32)], axis=0).T

    lat, cx = x, ctx
    for l in range(depth):
        last = l == depth - 1
        j = l // n_mixers
        mods = _ada(cond, w_ada, b_ada, nb + 1, l)
        if l % n_mixers == 0:
            w_in, w_out = bf("hg_in", j), bf("hg_out", j)
            s0 = jnp.zeros((nb, 2, d // HG_PAIR, HG_PAIR, HG_PAIR), F32)
            lb_logits = jnp.swapaxes(hg_lb_logits, 0, 1)
            hg_in = functools.partial(_hg_in, mods=mods, norm_g=norm_g[l, 0], w_in=w_in,
                                      lb_logits=lb_logits, layer_j=j, n_lat=nb)
            pc, _ = hg_in(cx, ctx=True)
            (oc_f, oc_b, s_ctx), _ = _hg_scan(pc[0], pc[1], pc[2], pc[3], s0)
            pl_ = hosted(functools.partial(hg_in, lat, ctx=False), ("ffn_down", l))
            ol_f, ol_b, _ = hosted(functools.partial(_hg_scan, pl_[0], pl_[1], pl_[2], pl_[3], s_ctx),
                                   ("ffn_up", l))
            mix_lat = (ol_f, ol_b, pl_[4])
            mix_ctx = (oc_f, oc_b, pc[4])
            head_gain = hg_gnorm[j]
        else:
            w_in, w_out = bf("lru_in", j), bf("lru_out", j)
            scan = functools.partial(_lru_scan, conv_w=lru_conv_w[j], conv_b=lru_conv_b[j],
                                     wa=bf("lru_wa", j), wx=bf("lru_wx", j),
                                     ba=lru_ba[j], bx=lru_bx[j], lam=lru_lambda[j])
            lru_in = functools.partial(_lru_in, mods=mods, norm_g=norm_g[l, 0], w_in=w_in, n_lat=nb)
            wc = LRU_COL_BLOCK
            t_ctx = cx.shape[1]
            flip = lambda a: jnp.swapaxes(a.reshape(nb, wc, t_ctx // wc, d), 1, 2).reshape(nb, t_ctx, d)
            unflip = lambda a: jnp.swapaxes(a.reshape(nb, t_ctx // wc, wc, d), 1, 2).reshape(nb, t_ctx, d)
            (gg_c, xr_c), _ = lru_in(cx, ctx=True)
            hc_f, hc_b, h_ctx = scan(flip(xr_c), h0=jnp.zeros((2, nb, 1, d), F32), grid_w=wc)
            gg_l, xr_l = hosted(functools.partial(lru_in, lat, ctx=False), ("ffn_down", l))
            hl_f, hl_b, _ = scan(xr_l, h0=h_ctx, grid_w=GRID_W)
            mix_lat = (hl_f, hl_b, gg_l)
            mix_ctx = (hc_f, hc_b, gg_c)
            head_gain = None
        mix_ffn = functools.partial(_mix_ffn, mods=mods, head_gain=head_gain, w_out=w_out,
                                    norm_g=norm_g[l, 1], w_up=bf("ffn_up", l), conv_w=ffn_conv_w[l],
                                    conv_b=ffn_conv_b[l], w_down=bf("ffn_down", l), final_g=final_g, n_lat=nb)
        nxt = [] if last else mixer_keys(l + 1) + ([("ffn_up", l + 1)] if (l + 1) % n_mixers else [])
        lat = hosted(functools.partial(mix_ffn, *mix_lat, lat, ctx=False, final_norm=last), *nxt)
        if not last:
            if head_gain is None:
                mix_ctx = (unflip(mix_ctx[0]), unflip(mix_ctx[1]), mix_ctx[2])
            cx, _ = mix_ffn(*mix_ctx, cx, ctx=True, final_norm=False)
    return lat
```

```python
import functools

import jax
import jax.numpy as jnp
from jax import lax
from jax.experimental import pallas as pl
from jax.experimental.pallas import tpu as pltpu

F32 = jnp.float32
BF16 = jnp.bfloat16

EPS = 1e-6
GRID_W = 64
HG_DK = 128
CHUNK = 64
LRU_BLOCKS = 4
LRU_C = 8.0
LRU_CONV_W = 4
FFN_CONV_W = 3

SUBLANES = 8
LANES = 128
BF16_ROWS = 16
MXU_N = 256
VMEM_LIMIT_BYTES = 56 * 1024 * 1024
COND_ROWS = SUBLANES


def _cparams(*sem):
    return pltpu.CompilerParams(dimension_semantics=sem, vmem_limit_bytes=VMEM_LIMIT_BYTES)


def _resident(shape):
    zeros = (0,) * len(shape)
    return pl.BlockSpec(shape, lambda *_: zeros, pipeline_mode=pl.Buffered(1))


IN_PROJ_ROW_TILE = 1024


def _row_tile(t, largest):
    tm = largest
    while t % tm:
        tm //= 2
    return tm


def _sigmoid(x):
    return 0.5 + 0.5 * jnp.tanh(0.5 * x)


def _silu(x):
    return x * _sigmoid(x)


def _gelu_tanh(x):
    return 0.5 * x * (1.0 + jnp.tanh(0.7978845608028654 * (x + 0.044715 * (x * x * x))))


def _rms(x, g):
    return x * lax.rsqrt(jnp.mean(x * x, axis=-1, keepdims=True) + EPS) * g


def _mod_row(mod_ref, ctx, n_lat):
    if ctx:
        return mod_ref[0, n_lat:n_lat + 1, :]
    return mod_ref[0, pl.ds(pl.program_id(0), 1), :]


def _mod_spec(k, d):
    return pl.BlockSpec((1, COND_ROWS, d), lambda *_: (0, 0, k))


CAST_BLOCK_BYTES = 4 * 1024 * 1024


def _cast_kernel(w_ref, o_ref, *, scale):
    w = w_ref[0]
    o_ref[...] = (w if scale == 1.0 else w * scale).astype(o_ref.dtype)


def _layer_bf16(w, layer, scale=1.0):
    shape = w.shape[1:]
    cols = shape[-1]
    rows = w[0].size // cols
    tr = rows
    while tr * cols * 4 > CAST_BLOCK_BYTES and tr % (2 * BF16_ROWS) == 0:
        tr //= 2
    out = pl.pallas_call(
        functools.partial(_cast_kernel, scale=scale),
        out_shape=jax.ShapeDtypeStruct((rows, cols), BF16),
        grid=(rows // tr,),
        in_specs=[pl.BlockSpec((1, tr, cols), lambda i: (layer, i, 0))],
        out_specs=pl.BlockSpec((tr, cols), lambda i: (i, 0)),
        compiler_params=_cparams("parallel"),
        name="cast_bf16",
    )(w.reshape(w.shape[0], rows, cols))
    return out.reshape(shape)


class _SideCast:
    def __init__(self, w, layer, scale=1.0):
        self.shape = w.shape[1:]
        self.cols = w.shape[-1]
        self.rows = w[0].size // self.cols
        self.w = w.reshape(w.shape[0], self.rows, self.cols)
        self.layer = layer
        self.scale = scale

    def specs(self, n_steps, step):
        slab, rem = divmod(self.rows, n_steps)
        assert rem == 0 and slab % BF16_ROWS == 0, (self.rows, n_steps)
        layer = self.layer
        return ([pl.BlockSpec((1, slab, self.cols), lambda *ids: (layer, step(*ids), 0))],
                pl.BlockSpec((slab, self.cols), lambda *ids: (step(*ids), 0)),
                jax.ShapeDtypeStruct((self.rows, self.cols), BF16))

    def operands(self):
        return [self.w]

    def emit(self, in_refs, o_ref):
        w = in_refs[0][0]
        o_ref[...] = (w if self.scale == 1.0 else w * self.scale).astype(o_ref.dtype)

    def finish(self, out):
        return out.reshape(self.shape)


def _with_side_jobs(body, n_in, n_out, jobs, job_n_in):
    n_ji = sum(job_n_in)
    n_j = len(jobs)

    def kernel(*refs):
        ins, j_in = refs[:n_in], refs[n_in:n_in + n_ji]
        outs = refs[n_in + n_ji:n_in + n_ji + n_out]
        j_out = refs[n_in + n_ji + n_out:n_in + n_ji + n_out + n_j]
        at = 0
        for job, k, o_ref in zip(jobs, job_n_in, j_out):
            job.emit(j_in[at:at + k], o_ref)
            at += k
        body(*ins, *outs, *refs[n_in + n_ji + n_out + n_j:])

    return kernel


def _side_cast_call(body, jobs, n_steps, step, *, out_shape, in_specs, out_specs, args, **kw):
    spec3 = [job.specs(n_steps, step) for job in jobs]
    outs = pl.pallas_call(
        _with_side_jobs(body, len(in_specs), len(out_shape), jobs, [len(s[0]) for s in spec3]),
        out_shape=tuple(out_shape) + tuple(s[2] for s in spec3),
        in_specs=list(in_specs) + [spec for s in spec3 for spec in s[0]],
        out_specs=tuple(out_specs) + tuple(s[1] for s in spec3),
        **kw,
    )(*args, *[a for job in jobs for a in job.operands()])
    n = len(out_shape)
    return outs[:n], [job.finish(o) for o, job in zip(outs[n:], jobs)]


def _ada_kernel(ct_ref, w_ref, b_ref, o_ref, *, n_cond):
    ct = ct_ref[...]
    s = _silu(ct)
    w = w_ref[0]
    o_ref[0] = jnp.zeros(o_ref.shape[1:], F32) + b_ref[0]
    for m in range(n_cond):
        o_ref[0, m:m + 1, :] = jnp.sum(w * s[:, m:m + 1], axis=0, keepdims=True) + b_ref[0]


def _ada(cond_t, w_ada, b_ada, n_cond, layer):
    depth, d, n = w_ada.shape
    tn = 1024
    return pl.pallas_call(
        functools.partial(_ada_kernel, n_cond=n_cond),
        out_shape=jax.ShapeDtypeStruct((1, COND_ROWS, n), F32),
        grid=(n // tn,),
        in_specs=[
            pl.BlockSpec((d, COND_ROWS), lambda j: (0, 0)),
            pl.BlockSpec((1, d, tn), lambda j: (layer, 0, j)),
            pl.BlockSpec((1, 1, tn), lambda j: (layer, 0, j)),
        ],
        out_specs=pl.BlockSpec((1, COND_ROWS, tn), lambda j: (0, 0, j)),
        compiler_params=_cparams("parallel"),
        name="ada",
    )(cond_t, w_ada, b_ada.reshape(depth, 1, n))


def _split2(x):
    hi = x.astype(BF16)
    return hi, (x - hi.astype(F32)).astype(BF16)


def _hg_in_kernel(x_ref, g_ref, sh_ref, sc_ref, w_ref, lbl_ref, q_ref, lff_ref, lfb_ref, v_ref, gs_ref,
                  *, ctx, n_lat, layer_j):
    d = x_ref.shape[-1]
    sh = _mod_row(sh_ref, ctx, n_lat)
    sc = _mod_row(sc_ref, ctx, n_lat)
    h = (_rms(x_ref[0], g_ref[...] * (1.0 + sc)) + sh).astype(BF16)
    nc = 2 * MXU_N
    for c0 in range(0, 5 * d, nc):
        part, p0 = divmod(c0, d)
        z = jnp.dot(h, w_ref[:, c0:c0 + nc], preferred_element_type=F32)
        cols = slice(p0, p0 + nc)
        if part == 0:
            q_ref[0, :, cols] = _silu(z).astype(BF16)
        elif part in (1, 2):
            lg = lbl_ref[part - 1, :, cols]
            e = jnp.exp(lg - jnp.max(lg, axis=0, keepdims=True))
            lb = jnp.sum(e[:layer_j + 1], axis=0, keepdims=True) / jnp.sum(e, axis=0, keepdims=True)
            f = lb + (1.0 - lb) * _sigmoid(z)
            (lff_ref if part == 1 else lfb_ref)[0, :, cols] = jnp.log(f)
        elif part == 3:
            v_ref[0, :, cols] = z.astype(BF16)
        else:
            gs_ref[0, :, cols] = _silu(z).astype(BF16)


def _hg_in(x, mods, norm_g, w_in, lb_logits, layer_j, ctx, n_lat, casts=()):
    b, t, d = x.shape
    tm = _row_tile(t, IN_PROJ_ROW_TILE)
    nt = t // tm
    tile = pl.BlockSpec((1, tm, d), lambda bi, i: (bi, i, 0))
    sds = lambda dt: jax.ShapeDtypeStruct((b, t, d), dt)
    return _side_cast_call(
        functools.partial(_hg_in_kernel, ctx=ctx, n_lat=n_lat, layer_j=layer_j),
        casts, b * nt, lambda bi, i: bi * nt + i,
        out_shape=(sds(BF16), sds(F32), sds(F32), sds(BF16), sds(BF16)),
        grid=(b, nt),
        in_specs=[tile, _resident((1, d)), _mod_spec(0, d), _mod_spec(1, d),
                  _resident(w_in.shape), _resident(lb_logits.shape)],
        out_specs=(tile, tile, tile, tile, tile),
        compiler_params=_cparams("parallel", "parallel"),
        name="hg_in",
        args=(x, norm_g.reshape(1, d), mods, mods, w_in, lb_logits))


HG_PAIR = 2 * HG_DK
HG_SCAN_BLOCK = 512


def _hg_scan_kernel(qf_ref, lf_ref, vf_ref, qb_ref, lb_ref, vb_ref, s0_ref, of_ref, ob_ref, sfin_ref,
                    st_ref, st16_ref, qe_ref, qs_ref, kebd_ref, kdbd_ref, vbd_ref, dec_ref):
    i = pl.program_id(1)
    tb, d = qf_ref.shape[1], qf_ref.shape[2]
    n_chunks = tb // CHUNK
    n_pairs = d // HG_PAIR

    @pl.when(i == 0)
    def _():
        st_ref[...] = s0_ref[0]
        for dn in range(2):
            for p in range(n_pairs):
                st16_ref[dn, p] = s0_ref[0, dn, p].T.astype(BF16)
        kebd_ref[...] = jnp.zeros(kebd_ref.shape, BF16)
        kdbd_ref[...] = jnp.zeros(kdbd_ref.shape, BF16)
        vbd_ref[...] = jnp.zeros(vbd_ref.shape, BF16)

    grp = min(tb, MXU_N)
    row = lax.broadcasted_iota(jnp.int32, (grp, grp), 0)
    col = lax.broadcasted_iota(jnp.int32, (grp, grp), 1)
    same_chunk = (row // CHUNK) == (col // CHUNK)
    r_in = lax.broadcasted_iota(jnp.int32, (CHUNK, 2 * CHUNK), 0)
    c_in = lax.broadcasted_iota(jnp.int32, (CHUNK, 2 * CHUNK), 1) % CHUNK
    dirs = (
        (qf_ref, lf_ref, vf_ref, of_ref, same_chunk & (row >= col), r_in >= c_in, CHUNK // 2, CHUNK - 1),
        (qb_ref, lb_ref, vb_ref, ob_ref, same_chunk & (row <= col), r_in <= c_in, CHUNK // 2 - 1, 0),
    )

    for dn, (q_ref, lfd_ref, v_ref, _, keep_blk, _, ref_row, last_row) in enumerate(dirs):
        tri = jnp.where(keep_blk, 1.0, 0.0).astype(BF16)
        lf = lfd_ref[0]
        b = jnp.concatenate(
            [sum(jnp.dot(tri, part, preferred_element_type=F32) for part in _split2(lf[r0:r0 + grp]))
             for r0 in range(0, tb, grp)], axis=0)
        for c in range(n_chunks):
            rows = slice(c * CHUNK, (c + 1) * CHUNK)
            bc = b[rows]
            b_mid = bc[ref_row:ref_row + 1, :]
            b_end = bc[last_row:last_row + 1, :]
            e_q = jnp.exp(bc - b_mid)
            qe = q_ref[0, rows, :].astype(F32) * e_q
            ke = (1.0 - jnp.exp(lf[rows])) * (1.0 / e_q)
            qe_ref[dn, rows, :] = qe.astype(BF16)
            qs_ref[dn, rows, :] = (qe * jnp.exp(b_mid)).astype(BF16)
            ke16 = ke.astype(BF16)
            kd16 = (ke * jnp.exp(b_end - b_mid)).astype(BF16)
            for h in range(2 * n_pairs):
                p, hh = divmod(h, 2)
                cols = slice(h * HG_DK, (h + 1) * HG_DK)
                blk = (slice(hh * CHUNK, (hh + 1) * CHUNK), slice(hh * HG_DK, (hh + 1) * HG_DK))
                kebd_ref[dn, c, p, blk[0], blk[1]] = ke16[:, cols]
                kdbd_ref[dn, c, p, blk[0], blk[1]] = kd16[:, cols]
                vbd_ref[dn, c, p, blk[0], blk[1]] = v_ref[0, rows, cols]
            dec_ref[dn, c:c + 1, :] = jnp.exp(b_end)

    for c in range(n_chunks):
        for dn, (_, _, v_ref, o_ref, _, keep, _, _) in enumerate(dirs):
            cc = c if dn == 0 else n_chunks - 1 - c
            rows = slice(cc * CHUNK, (cc + 1) * CHUNK)
            for p in range(n_pairs):
                pc = slice(p * HG_PAIR, (p + 1) * HG_PAIR)
                scores = lax.dot_general(qe_ref[dn, rows, pc], kebd_ref[dn, cc, p], (((1,), (1,)), ((), ())),
                                         preferred_element_type=F32)
                scores = jnp.where(keep, scores, 0.0).astype(BF16)
                o = jnp.dot(scores, vbd_ref[dn, cc, p], preferred_element_type=F32)
                o = o + jnp.dot(qs_ref[dn, rows, pc], st16_ref[dn, p], preferred_element_type=F32)
                o_ref[0, rows, pc] = o.astype(BF16)
                v_stack = jnp.concatenate(
                    [v_ref[0, rows, p * HG_PAIR:p * HG_PAIR + HG_DK],
                     v_ref[0, rows, p * HG_PAIR + HG_DK:(p + 1) * HG_PAIR]], axis=0)
                kv = lax.dot_general(v_stack, kdbd_ref[dn, cc, p], (((0,), (0,)), ((), ())),
                                     preferred_element_type=F32)
                for hh in range(2):
                    sl = slice(hh * HG_DK, (hh + 1) * HG_DK)
                    cols = slice(p * HG_PAIR + hh * HG_DK, p * HG_PAIR + (hh + 1) * HG_DK)
                    new = st_ref[dn, p, sl, sl] * dec_ref[dn, cc:cc + 1, cols] + kv[:, sl]
                    st_ref[dn, p, sl, sl] = new
                    st16_ref[dn, p, sl, sl] = new.T.astype(BF16)

    @pl.when(i == pl.num_programs(1) - 1)
    def _():
        sfin_ref[0] = st_ref[...]


def _hg_scan(q, lf_f, lf_b, v, s0, casts=()):
    b, t, d = q.shape
    tb = _row_tile(t, HG_SCAN_BLOCK)
    nblk = t // tb
    fwd = pl.BlockSpec((1, tb, d), lambda bi, i: (bi, i, 0))
    bwd = pl.BlockSpec((1, tb, d), lambda bi, i: (bi, nblk - 1 - i, 0))
    st_spec = pl.BlockSpec((1,) + s0.shape[1:], lambda bi, i: (bi, 0, 0, 0, 0))
    blockdiag = pltpu.VMEM((2, tb // CHUNK, d // HG_PAIR, 2 * CHUNK, HG_PAIR), BF16)
    return _side_cast_call(
        _hg_scan_kernel, casts, b * nblk, lambda bi, i: bi * nblk + i,
        out_shape=(jax.ShapeDtypeStruct((b, t, d), BF16), jax.ShapeDtypeStruct((b, t, d), BF16),
                   jax.ShapeDtypeStruct(s0.shape, F32)),
        grid=(b, nblk),
        in_specs=[fwd, fwd, fwd, bwd, bwd, bwd, st_spec],
        out_specs=(fwd, bwd, st_spec),
        scratch_shapes=[pltpu.VMEM(s0.shape[1:], F32), pltpu.VMEM(s0.shape[1:], BF16),
                        pltpu.VMEM((2, tb, d), BF16), pltpu.VMEM((2, tb, d), BF16),
                        blockdiag, blockdiag, blockdiag, pltpu.VMEM((2, tb // CHUNK, d), F32)],
        compiler_params=_cparams("parallel", "arbitrary"),
        name="hg_scan",
        args=(q, lf_f, v, q, lf_b, v, s0))


def _lru_in_kernel(x_ref, g_ref, sh_ref, sc_ref, w_ref, gg_ref, xr_ref, *, ctx, n_lat):
    d = x_ref.shape[-1]
    sh = _mod_row(sh_ref, ctx, n_lat)
    sc = _mod_row(sc_ref, ctx, n_lat)
    h = (_rms(x_ref[0], g_ref[...] * (1.0 + sc)) + sh).astype(BF16)
    nc = 2 * MXU_N
    for c0 in range(0, 2 * d, nc):
        part, p0 = divmod(c0, d)
        z = jnp.dot(h, w_ref[:, c0:c0 + nc], preferred_element_type=F32)
        if part == 0:
            gg_ref[0, :, p0:p0 + nc] = _gelu_tanh(z).astype(BF16)
        else:
            xr_ref[0, :, p0:p0 + nc] = z


def _lru_in(x, mods, norm_g, w_in, ctx, n_lat, casts=()):
    b, t, d = x.shape
    tm = _row_tile(t, IN_PROJ_ROW_TILE)
    nt = t // tm
    tile = pl.BlockSpec((1, tm, d), lambda bi, i: (bi, i, 0))
    return _side_cast_call(
        functools.partial(_lru_in_kernel, ctx=ctx, n_lat=n_lat),
        casts, b * nt, lambda bi, i: bi * nt + i,
        out_shape=(jax.ShapeDtypeStruct((b, t, d), BF16), jax.ShapeDtypeStruct((b, t, d), F32)),
        grid=(b, nt),
        in_specs=[tile, _resident((1, d)), _mod_spec(0, d), _mod_spec(1, d),
                  _resident(w_in.shape)],
        out_specs=(tile, tile),
        compiler_params=_cparams("parallel", "parallel"),
        name="lru_in",
        args=(x, norm_g.reshape(1, d), mods, mods, w_in))


LRU_COL_BLOCK = 2 * BF16_ROWS
LRU_GATE_ROWS = MXU_N
LRU_SCAN_SEGS = 1
LRU_SCAN_UNROLL = 8
LOG2_E = 1.4426950408889634
LN_2 = 0.6931471805599453
HALO = SUBLANES


def _softplus(x):
    y = jnp.exp(-jnp.abs(x))
    u = 1.0 + y
    log1p = jnp.where(u == 1.0, y, jnp.log(u) * (y / (u - 1.0)))
    return jnp.maximum(x, 0.0) + log1p


def _lru_scan_kernel(xf_ref, xfp_ref, xfn_ref, xb_ref, xbp_ref, xbn_ref, cw_ref, cb_ref, wa_ref, wx_ref,
                     ba_ref, bx_ref, lam_ref, h0_ref, hf_ref, hb_ref, hfin_ref, xpad_ref, a_ref, u_ref,
                     carry_ref):
    j = pl.program_id(2)
    nj = pl.num_programs(2)
    _, nr, wb, cb = xf_ref.shape

    @pl.when(j == 0)
    def _():
        carry_ref[...] = h0_ref[:, 0]

    col = lax.broadcasted_iota(jnp.int32, (wb, cb), 0)
    left = LRU_CONV_W - 1 - (LRU_CONV_W - 1) // 2
    right = LRU_CONV_W - 1 - left
    rc = LRU_GATE_ROWS // wb
    n_ch = nr // rc
    x_refs = (xf_ref, xb_ref)
    for dn, (x_ref, xp_ref, xn_ref, jb) in enumerate(((xf_ref, xfp_ref, xfn_ref, j),
                                                       (xb_ref, xbp_ref, xbn_ref, nj - 1 - j))):
        last_end = min(n_ch - 1, 1)
        for k in range(left):
            edge = jnp.where(jb > 0, xp_ref[0, HALO - left + k, wb - 1:wb, :], 0.0)
            xpad_ref[dn, 0, k] = jnp.where(col == 0, edge, pltpu.roll(x_ref[0, nr - left + k], 1, axis=0))
        for k in range(right):
            edge = jnp.where(jb < nj - 1, xn_ref[0, k, 0:1, :], 0.0)
            xpad_ref[dn, last_end, left + rc + k] = jnp.where(col == wb - 1, edge,
                                                              pltpu.roll(x_ref[0, k], wb - 1, axis=0))
        if n_ch == 1:
            xpad_ref[dn, 0, left:left + rc] = x_ref[0]
        else:
            xpad_ref[dn, 0, left:] = x_ref[0, 0:rc + right]
            xpad_ref[dn, 1, 0:left + rc] = x_ref[0, nr - rc - left:nr]

    cw = [cw_ref[k:k + 1, :] for k in range(LRU_CONV_W)]
    cbias = cb_ref[...]
    k2 = [(-0.5 * LRU_C * LOG2_E) * _softplus(-lam_ref[dn]) for dn in range(2)]
    half_ba = [0.5 * ba_ref[dn] for dn in range(2)]
    half_bx = [0.5 * bx_ref[dn] for dn in range(2)]

    def gates(dn, rows, tap):
        xc = cbias + sum(cw[k] * tap(k) for k in range(LRU_CONV_W))
        xc = xc.reshape(rc * wb, cb)
        xcb = xc.astype(BF16)
        tr = jnp.tanh(jnp.dot(xcb, wa_ref[dn, 0], preferred_element_type=F32) + half_ba[dn])
        ti = jnp.tanh(jnp.dot(xcb, wx_ref[dn, 0], preferred_element_type=F32) + half_bx[dn])
        log2_a = k2[dn] * tr + k2[dn]
        a = jnp.exp2(log2_a)
        q2 = (a * a * 0.25 + 0.25) * jnp.tanh(log2_a * (-LN_2))
        half_mult = jnp.where(q2 > 0.0, q2 * lax.rsqrt(q2), 0.0)
        a_ref[dn, rows] = a.reshape(rc, wb, cb)
        u_ref[dn, rows] = (half_mult * (xc * ti + xc)).reshape(rc, wb, cb)

    for end, r0 in ((0, 0), (1, nr - rc))[:min(n_ch, 2)]:
        for dn in range(2):
            gates(dn, slice(r0, r0 + rc), lambda k, dn=dn, end=end: xpad_ref[dn, end, k:k + rc])

    def interior_chunk(ci, carry):
        r0 = pl.multiple_of(ci * rc, rc)
        for dn in range(2):
            gates(dn, pl.ds(r0, rc), lambda k, dn=dn: x_refs[dn][0, pl.ds(r0 - left + k, rc)])
        return carry

    if n_ch > 2:
        lax.fori_loop(1, n_ch - 1, interior_chunk, 0)

    n_seg = LRU_SCAN_SEGS
    seg = nr // n_seg
    unroll = min(seg, LRU_SCAN_UNROLL)
    row_of = lambda dn, s, rr: s * seg + (rr if dn == 0 else seg - 1 - rr)
    units = [(dn, s) for dn in range(2) for s in range(n_seg)]

    def sweep1(rr, carry):
        out = []
        for (dn, s), (h, p) in zip(units, carry):
            a = a_ref[dn, row_of(dn, s, rr)]
            out.append((a * h + u_ref[dn, row_of(dn, s, rr)], a * p))
        return tuple(out)

    zero = jnp.zeros((wb, cb), F32)
    one = jnp.ones((wb, cb), F32)
    ends = lax.fori_loop(0, seg, sweep1, ((zero, one),) * len(units), unroll=unroll)

    entry = {}
    for dn in range(2):
        c_in = carry_ref[dn]
        wls = range(wb) if dn == 0 else range(wb - 1, -1, -1)
        segs = range(n_seg) if dn == 0 else range(n_seg - 1, -1, -1)
        for s in segs:
            entry[dn, s] = zero
        for wl in wls:
            for s in segs:
                h_end, p_end = ends[units.index((dn, s))]
                entry[dn, s] = jnp.where(col == wl, c_in, entry[dn, s])
                c_in = p_end[wl:wl + 1, :] * c_in + h_end[wl:wl + 1, :]
        carry_ref[dn] = c_in

    o_refs = (hf_ref, hb_ref)

    def sweep2(rr, hs):
        out = []
        for (dn, s), h in zip(units, hs):
            r = row_of(dn, s, rr)
            h = a_ref[dn, r] * h + u_ref[dn, r]
            o_refs[dn][0, r] = h.astype(o_refs[dn].dtype)
            out.append(h)
        return tuple(out)

    lax.fori_loop(0, seg, sweep2, tuple(entry[u] for u in units), unroll=unroll)

    @pl.when(j == nj - 1)
    def _():
        hfin_ref[:, 0] = carry_ref[...]


def _lru_scan(xr, conv_w, conv_b, wa, wx, ba, bx, lam, h0, grid_w):
    b, t, c = xr.shape
    nr = t // grid_w
    cb = c // LRU_BLOCKS
    wb = LRU_COL_BLOCK
    nj = grid_w // wb
    x4 = xr.reshape(b, nr, grid_w, c)
    up = lambda j: j
    down = lambda j: nj - 1 - j
    main = lambda cj: pl.BlockSpec((1, nr, wb, cb), lambda bi, n, j: (bi, 0, cj(j), n))
    prev = lambda cj: pl.BlockSpec((1, HALO, wb, cb),
                                   lambda bi, n, j: (bi, nr // HALO - 1, jnp.maximum(cj(j) - 1, 0), n))
    nxt = lambda cj: pl.BlockSpec((1, HALO, wb, cb),
                                  lambda bi, n, j: (bi, 0, jnp.minimum(cj(j) + 1, nj - 1), n))
    chan = lambda rows: pl.BlockSpec((rows, cb), lambda bi, n, j: (0, n))
    per_dir = pl.BlockSpec((2, 1, cb), lambda bi, n, j: (0, 0, n))
    gate_w = pl.BlockSpec((2, 1, cb, cb), lambda bi, n, j: (0, n, 0, 0))
    state = pl.BlockSpec((2, 1, 1, cb), lambda bi, n, j: (0, bi, 0, n))
    hf, hb, hfin = pl.pallas_call(
        _lru_scan_kernel,
        out_shape=(jax.ShapeDtypeStruct(x4.shape, BF16), jax.ShapeDtypeStruct(x4.shape, BF16),
                   jax.ShapeDtypeStruct(h0.shape, F32)),
        grid=(b, LRU_BLOCKS, nj),
        in_specs=[main(up), prev(up), nxt(up), main(down), prev(down), nxt(down),
                  chan(LRU_CONV_W), chan(1), gate_w, gate_w, per_dir, per_dir, per_dir, state],
        out_specs=(main(up), main(down), state),
        scratch_shapes=[pltpu.VMEM((2, 2, LRU_GATE_ROWS // wb + LRU_CONV_W - 1, wb, cb), F32)]
        + [pltpu.VMEM((2, nr, wb, cb), F32)] * 2 + [pltpu.VMEM((2, 1, cb), F32)],
        compiler_params=_cparams("parallel", "parallel", "arbitrary"),
        name="lru_scan",
    )(x4, x4, x4, x4, x4, x4, conv_w, conv_b.reshape(1, c), wa, wx,
      ba.reshape(2, 1, c), bx.reshape(2, 1, c), lam.reshape(2, 1, c), h0)
    return hf.reshape(b, t, c), hb.reshape(b, t, c), hfin


FFN_ROW_TILE = 512


def _mix_ffn_kernel(af_ref, ab_ref, gs_ref, xm_ref, afp_ref, abp_ref, gsp_ref, xp_ref, afn_ref, abn_ref,
                    gsn_ref, xn_ref, gain_ref, wo_ref, gate1_ref, g_ref, sh_ref, sc_ref, gate_ref, wup_ref,
                    cw_ref, cb_ref, wdn_ref, fg_ref, out_ref, y_ref, lat_ref, h_ref, act_ref,
                    *, head_norm, ctx, n_lat, final_norm):
    i = pl.program_id(1)
    tm, d = xm_ref.shape[1], xm_ref.shape[2]
    dff = wdn_ref.shape[0]
    lo = slice(BF16_ROWS - HALO, BF16_ROWS)
    hi = slice(0, HALO)

    def readout_in(af, ab, gs):
        o = af.astype(F32) + ab.astype(F32)
        if head_norm:
            dv = gain_ref.shape[-1]
            o = jnp.concatenate([_rms(o[:, c0:c0 + dv], gain_ref[...]) for c0 in range(0, d, dv)], axis=-1)
        return o * gs.astype(F32)

    g = g_ref[...]
    sh = _mod_row(sh_ref, ctx, n_lat)
    sc = _mod_row(sc_ref, ctx, n_lat)
    g_sc = g * (1.0 + sc)
    mod = lambda x: _rms(x, g_sc) + sh
    rows = tm + 2 * HALO
    y_ref[...] = jnp.concatenate(
        [readout_in(afp_ref[0, lo], abp_ref[0, lo], gsp_ref[0, lo]),
         readout_in(af_ref[0], ab_ref[0], gs_ref[0]),
         readout_in(afn_ref[0, hi], abn_ref[0, hi], gsn_ref[0, hi])], axis=0).astype(BF16)
    x_all = jnp.concatenate([xp_ref[0, lo], xm_ref[0], xn_ref[0, hi]], axis=0)
    lat_ref[...] = x_all + _mod_row(gate1_ref, ctx, n_lat) * jnp.dot(y_ref[...], wo_ref[...],
                                                                     preferred_element_type=F32)
    hp = jnp.where(i > 0, mod(lat_ref[0:HALO]), 0.0)
    hn = jnp.where(i < pl.num_programs(1) - 1, mod(lat_ref[HALO + tm:]), 0.0)
    h_ref[...] = jnp.concatenate([hp, mod(lat_ref[HALO:HALO + tm]), hn], axis=0).astype(BF16)
    nc = MXU_N
    for c0 in range(0, dff, nc):
        halves = []
        for base in (c0, dff + c0):
            u = jnp.dot(h_ref[...], wup_ref[:, base:base + nc], preferred_element_type=F32)
            cols = slice(base, base + nc)
            conv = (cb_ref[:, cols]
                    + cw_ref[0:1, cols] * pltpu.roll(u, 1, axis=0)[HALO:HALO + tm]
                    + cw_ref[1:2, cols] * u[HALO:HALO + tm]
                    + cw_ref[2:3, cols] * pltpu.roll(u, rows - 1, axis=0)[HALO:HALO + tm])
            halves.append(conv)
        act_ref[:, c0:c0 + nc] = (_silu(halves[0]) * halves[1]).astype(BF16)
    half = tm // 2
    for r0 in (0, half):
        z = jnp.dot(act_ref[r0:r0 + half], wdn_ref[...], preferred_element_type=F32)
        y = lat_ref[HALO + r0:HALO + r0 + half] + _mod_row(gate_ref, ctx, n_lat) * z
        out_ref[0, r0:r0 + half] = _rms(y, fg_ref[...]) if final_norm else y


def _mix_ffn(a_f, a_b, gs, x, mods, head_gain, w_out, norm_g, w_up, conv_w, conv_b, w_down, final_g,
             ctx, n_lat, final_norm, casts=()):
    b, t, d = x.shape
    dff = w_down.shape[0]
    assert conv_w.shape[0] == FFN_CONV_W
    tm = _row_tile(t, FFN_ROW_TILE)
    nt = t // tm
    per = tm // BF16_ROWS
    tile = pl.BlockSpec((1, tm, d), lambda bi, i: (bi, i, 0))
    prev = pl.BlockSpec((1, BF16_ROWS, d), lambda bi, i: (bi, jnp.maximum(i * per - 1, 0), 0))
    nxt = pl.BlockSpec((1, BF16_ROWS, d), lambda bi, i: (bi, jnp.minimum((i + 1) * per, t // BF16_ROWS - 1), 0))
    head_norm = head_gain is not None
    gain = head_gain.reshape(1, -1) if head_norm else jnp.ones((1, LANES), F32)
    rows = tm + 2 * HALO
    (out,), cast_w = _side_cast_call(
        functools.partial(_mix_ffn_kernel, head_norm=head_norm, ctx=ctx, n_lat=n_lat, final_norm=final_norm),
        casts, b * nt, lambda bi, i: bi * nt + i,
        out_shape=(jax.ShapeDtypeStruct((b, t, d), F32),),
        grid=(b, nt),
        in_specs=[tile, tile, tile, tile, prev, prev, prev, prev, nxt, nxt, nxt, nxt,
                  _resident(gain.shape), _resident(w_out.shape), _mod_spec(2, d),
                  _resident((1, d)), _mod_spec(3, d), _mod_spec(4, d), _mod_spec(5, d),
                  _resident(w_up.shape), _resident(conv_w.shape), _resident((1, 2 * dff)),
                  _resident(w_down.shape), _resident((1, d))],
        out_specs=(tile,),
        scratch_shapes=[pltpu.VMEM((rows, d), BF16), pltpu.VMEM((rows, d), F32), pltpu.VMEM((rows, d), BF16),
                        pltpu.VMEM((tm, dff), BF16)],
        compiler_params=_cparams("parallel", "parallel"),
        name="mix_ffn",
        args=(a_f, a_b, gs, x, a_f, a_b, gs, x, a_f, a_b, gs, x, gain, w_out, mods,
              norm_g.reshape(1, d), mods, mods, mods, w_up, conv_w, conv_b.reshape(1, 2 * dff), w_down,
              final_g.reshape(1, d)))
    return out, cast_w


def kernel(x, c, ctx, c_ctx, w_ada, b_ada, norm_g, hg_w_in, hg_lb_logits, hg_gnorm, hg_w_out, lru_w_in,
           lru_conv_w, lru_conv_b, lru_wa, lru_ba, lru_wx, lru_bx, lru_lambda, lru_w_out, ffn_w_up,
           ffn_conv_w, ffn_conv_b, ffn_w_down, final_g):
    nb, _, d = x.shape
    depth = w_ada.shape[0]
    n_mixers = 2

    weights = {"hg_in": (hg_w_in, 1.0), "hg_out": (hg_w_out, 1.0), "lru_in": (lru_w_in, 1.0),
               "lru_out": (lru_w_out, 1.0), "lru_wa": (lru_wa, 0.5), "lru_wx": (lru_wx, 0.5),
               "ffn_up": (ffn_w_up, 1.0), "ffn_down": (ffn_w_down, 1.0)}
    ready = {}

    def bf(name, idx):
        if (name, idx) not in ready:
            ready[name, idx] = _layer_bf16(weights[name][0], idx, weights[name][1])
        return ready[name, idx]

    def hosted(call, *keys):
        keys = [k for k in keys if k not in ready]
        outs, cast_w = call(casts=[_SideCast(weights[n][0], i, weights[n][1]) for n, i in keys])
        ready.update(zip(keys, cast_w))
        return outs

    def mixer_keys(l):
        j = l // n_mixers
        names = ("hg_in", "hg_out") if l % n_mixers == 0 else ("lru_in", "lru_out", "lru_wa", "lru_wx")
        return [(n, j) for n in names]

    cond = jnp.concatenate([c, c_ctx[None, :], jnp.zeros((COND_ROWS - nb - 1, d), F32)], axis=0).T

    lat, cx = x, ctx
    for l in range(depth):
        last = l == depth - 1
        j = l // n_mixers
        mods = _ada(cond, w_ada, b_ada, nb + 1, l)
        if l % n_mixers == 0:
            w_in, w_out = bf("hg_in", j), bf("hg_out", j)
            s0 = jnp.zeros((nb, 2, d // HG_PAIR, HG_PAIR, HG_PAIR), F32)
            lb_logits = jnp.swapaxes(hg_lb_logits, 0, 1)
            hg_in = functools.partial(_hg_in, mods=mods, norm_g=norm_g[l, 0], w_in=w_in,
                                      lb_logits=lb_logits, layer_j=j, n_lat=nb)
            pc, _ = hg_in(cx, ctx=True)
            (oc_f, oc_b, s_ctx), _ = _hg_scan(pc[0], pc[1], pc[2], pc[3], s0)
            pl_ = hosted(functools.partial(hg_in, lat, ctx=False), ("ffn_down", l))
            ol_f, ol_b, _ = hosted(functools.partial(_hg_scan, pl_[0], pl_[1], pl_[2], pl_[3], s_ctx),
                                   ("ffn_up", l))
            mix_lat = (ol_f, ol_b, pl_[4])
            mix_ctx = (oc_f, oc_b, pc[4])
            head_gain = hg_gnorm[j]
        else:
            w_in, w_out = bf("lru_in", j), bf("lru_out", j)
            scan = functools.partial(_lru_scan, conv_w=lru_conv_w[j], conv_b=lru_conv_b[j],
                                     wa=bf("lru_wa", j), wx=bf("lru_wx", j),
                                     ba=lru_ba[j], bx=lru_bx[j], lam=lru_lambda[j])
            lru_in = functools.partial(_lru_in, mods=mods, norm_g=norm_g[l, 0], w_in=w_in, n_lat=nb)
            wc = LRU_COL_BLOCK
            t_ctx = cx.shape[1]
            flip = lambda a: jnp.swapaxes(a.reshape(nb, wc, t_ctx // wc, d), 1, 2).reshape(nb, t_ctx, d)
            unflip = lambda a: jnp.swapaxes(a.reshape(nb, t_ctx // wc, wc, d), 1, 2).reshape(nb, t_ctx, d)
            (gg_c, xr_c), _ = lru_in(cx, ctx=True)
            hc_f, hc_b, h_ctx = scan(flip(xr_c), h0=jnp.zeros((2, nb, 1, d), F32), grid_w=wc)
            gg_l, xr_l = hosted(functools.partial(lru_in, lat, ctx=False), ("ffn_down", l))
            hl_f, hl_b, _ = scan(xr_l, h0=h_ctx, grid_w=GRID_W)
            mix_lat = (hl_f, hl_b, gg_l)
            mix_ctx = (hc_f, hc_b, gg_c)
            head_gain = None
        mix_ffn = functools.partial(_mix_ffn, mods=mods, head_gain=head_gain, w_out=w_out,
                                    norm_g=norm_g[l, 1], w_up=bf("ffn_up", l), conv_w=ffn_conv_w[l],
                                    conv_b=ffn_conv_b[l], w_down=bf("ffn_down", l), final_g=final_g, n_lat=nb)
        nxt = [] if last else mixer_keys(l + 1) + ([("ffn_up", l + 1)] if (l + 1) % n_mixers else [])
        lat = hosted(functools.partial(mix_ffn, *mix_lat, lat, ctx=False, final_norm=last), *nxt)
        if not last:
            if head_gain is None:
                mix_ctx = (unflip(mix_ctx[0]), unflip(mix_ctx[1]), mix_ctx[2])
            cx, _ = mix_ffn(*mix_ctx, cx, ctx=True, final_norm=False)
    return lat
```
